```python
import math
import jax
import jax.numpy as jnp
from jax import lax
import numpy as np

D_MODEL = 4096
BATCH = 4
SEQ = 2048
DEPTH = 2
DEC_BATCH = 128
DEC_SEQ = 8
PAST_LEN = 16384
PAGE_SIZE = 128

F32 = jnp.float32
N_MIX_LAYERS = (DEPTH + 1) // 2
N_MLA_LAYERS = DEPTH // 2
RMS_EPS = 1e-6

GLA_HEADS = 4
GLA_VW = D_MODEL // 2
GLA_KW = GLA_VW // 2
GLA_DV = GLA_VW // GLA_HEADS
GLA_DK = GLA_KW // GLA_HEADS
GLA_GATE_RANK = 16
GLA_TAU = 16.0
GLA_CHUNK = 64

CONV_CH = D_MODEL // 2
CONV_WIDTH = 3

AB_SPLITS = (GLA_KW, GLA_KW, GLA_VW, GLA_GATE_RANK, GLA_VW, CONV_CH, CONV_CH, CONV_CH)
AB_IN_COLS = 2 * GLA_KW + 2 * GLA_VW + GLA_GATE_RANK + 3 * CONV_CH
AB_MIX = GLA_VW + CONV_CH

MLA_HEADS = D_MODEL // 128
Q_LORA = D_MODEL // 4
KV_LORA = 512
NOPE = 128
ROPE = 64
V_DIM = 128
MLA_IN_COLS = Q_LORA + KV_LORA + ROPE
MLA_SCALE = (NOPE + ROPE) ** -0.5
ROPE_THETA = 10000.0
Q_BLOCK = 128

N_GROUPS = 4
EXP_PER_GROUP = 8
N_EXPERTS = N_GROUPS * EXP_PER_GROUP
TOP_K = 2
D_EXPERT = D_MODEL // 4
MOE_BLOCK = 128

kernel_name = 'hybrid_gla_shortconv_mla_hmoe_step'


def rmsnorm(x, g):
    xf = x.astype(F32)
    y = xf * lax.rsqrt(jnp.mean(xf * xf, axis=-1, keepdims=True) + RMS_EPS)
    return (y * g.astype(F32)).astype(x.dtype)


def split_last(x, sizes):
    parts, start = [], 0
    for s in sizes:
        parts.append(x[..., start:start + s])
        start += s
    return parts


def gla_chunked(q, k, v, log_a, s0):
    Bn, L, H, DK = q.shape
    DV = v.shape[-1]
    C = GLA_CHUNK if L % GLA_CHUNK == 0 else L
    N = L // C

    def to_chunks(t):
        return t.astype(F32).reshape(Bn, N, C, H, t.shape[-1]).transpose(1, 0, 3, 2, 4)

    causal = jnp.tril(jnp.ones((C, C), dtype=bool))

    def step(S, inp):
        qc, kc, vc, gc = inp
        b = jnp.cumsum(gc, axis=2)
        b_last = b[:, :, -1:, :]
        q_dec = qc * jnp.exp(b)
        k_inv = kc * jnp.exp(-b)
        att = jnp.where(causal, jnp.einsum('bhtk,bhsk->bhts', q_dec, k_inv), 0.0)
        o = jnp.einsum('bhtk,bhkv->bhtv', q_dec, S) + jnp.einsum('bhts,bhsv->bhtv', att, vc)
        k_end = kc * jnp.exp(b_last - b)
        S = jnp.exp(b_last[:, :, 0, :])[..., None] * S + jnp.einsum('bhsk,bhsv->bhkv', k_end, vc)
        return S, o

    S, o = lax.scan(step, s0.astype(F32), (to_chunks(q), to_chunks(k), to_chunks(v), to_chunks(log_a)))
    o = o.transpose(1, 0, 3, 2, 4).reshape(Bn, L, H, DV)
    return o, S


def short_conv(u, buf, w):
    L = u.shape[1]
    full = jnp.concatenate([buf.astype(u.dtype), u], axis=1)
    z = full[:, 0:L] * w[0]
    for j in range(1, CONV_WIDTH):
        z = z + full[:, j:j + L] * w[j]
    return z, full[:, L:]


def mixer_ab(h, s_gla, s_conv, w_in, w_gate2, b_gate, g_out, conv_w, w_out):
    Bn, L, _ = h.shape
    q, k, v, g_lr, r, gate_b, gate_c, u = split_last(h @ w_in, AB_SPLITS)

    def heads(t, d):
        return t.reshape(Bn, L, GLA_HEADS, d)

    log_a = jax.nn.log_sigmoid((g_lr @ w_gate2).astype(F32) + b_gate.astype(F32)) / GLA_TAU
    o, s_gla_new = gla_chunked(heads(q, GLA_DK) * (GLA_DK ** -0.5), heads(k, GLA_DK),
                               heads(v, GLA_DV), heads(log_a, GLA_DK), s_gla)
    o = rmsnorm(o, g_out) * jax.nn.silu(heads(r, GLA_DV).astype(F32))
    o = o.reshape(Bn, L, GLA_VW).astype(h.dtype)
    z, s_conv_new = short_conv(gate_c * u, s_conv, conv_w)
    y_conv = (gate_b * z).astype(h.dtype)
    out = jnp.concatenate([o, y_conv], axis=-1) @ w_out
    return out, s_gla_new, s_conv_new


def rope_cos_sin(pos):
    inv = ROPE_THETA ** (-jnp.arange(0, ROPE, 2, dtype=F32) / ROPE)
    ang = pos.astype(F32)[:, None] * inv[None, :]
    return jnp.cos(ang), jnp.sin(ang)


def apply_rope(x, cos, sin):
    half = ROPE // 2
    xf = x.astype(F32)
    x1, x2 = xf[..., :half], xf[..., half:]
    return jnp.concatenate([x1 * cos - x2 * sin, x2 * cos + x1 * sin], axis=-1).astype(x.dtype)


def mla_project(h, pos, w_in, g_q, g_kv, w_uq):
    Bn, L, _ = h.shape
    c_q, c_kv, k_pe = split_last(h @ w_in, (Q_LORA, KV_LORA, ROPE))
    c_q = rmsnorm(c_q, g_q)
    c_kv = rmsnorm(c_kv, g_kv)
    q = (c_q @ w_uq).reshape(Bn, L, MLA_HEADS, NOPE + ROPE)
    cos, sin = rope_cos_sin(pos)
    q_pe = apply_rope(q[..., NOPE:], cos[:, None, :], sin[:, None, :])
    k_pe = apply_rope(k_pe, cos, sin)
    return q[..., :NOPE], q_pe, c_kv, k_pe


def mla_attend_prompt(q_nope, q_pe, c_kv, k_pe, w_uk, w_uv):
    Bn, S = q_nope.shape[:2]
    k_nope = jnp.einsum('bkc,chn->bkhn', c_kv, w_uk.reshape(KV_LORA, MLA_HEADS, NOPE))
    v = jnp.einsum('bkc,chv->bkhv', c_kv, w_uv.reshape(KV_LORA, MLA_HEADS, V_DIM))
    nb = S // Q_BLOCK

    def blocks(t):
        return jnp.moveaxis(t.reshape(Bn, nb, Q_BLOCK, *t.shape[2:]), 1, 0)

    k_pos = jnp.arange(S)

    def one_block(args):
        qn, qp, q_pos = args
        s = (jnp.einsum('bqhn,bkhn->bhqk', qn, k_nope)
             + jnp.einsum('bqhr,bkr->bhqk', qp, k_pe)).astype(F32) * MLA_SCALE
        s = jnp.where(q_pos[:, None] >= k_pos[None, :], s, -jnp.inf)
        p = jax.nn.softmax(s, axis=-1).astype(v.dtype)
        return jnp.einsum('bhqk,bkhv->bqhv', p, v)

    o = lax.map(one_block, (blocks(q_nope), blocks(q_pe), k_pos.reshape(nb, Q_BLOCK)))
    return jnp.moveaxis(o, 0, 1).reshape(Bn, S, MLA_HEADS * V_DIM)


def mla_attend_paged(q_nope, q_pe, c_kv, k_pe, w_uk, w_uv, cache_lat, cache_pe, page_table, li):
    Bn, T = q_nope.shape[:2]
    past = page_table.shape[1] * PAGE_SIZE
    q_lat = jnp.einsum('bthn,chn->bthc', q_nope, w_uk.reshape(KV_LORA, MLA_HEADS, NOPE))
    mask = jnp.arange(past + T)[None, :] <= (past + jnp.arange(T))[:, None]

    def one_seq(args):
        ql, qp, lat_new, pe_new, pages = args
        lat = jnp.concatenate([cache_lat[li, pages].reshape(past, KV_LORA).astype(lat_new.dtype), lat_new], 0)
        pe = jnp.concatenate([cache_pe[li, pages].reshape(past, ROPE).astype(pe_new.dtype), pe_new], 0)
        s = (jnp.einsum('thc,kc->htk', ql, lat) + jnp.einsum('thr,kr->htk', qp, pe)).astype(F32) * MLA_SCALE
        s = jnp.where(mask[None], s, -jnp.inf)
        p = jax.nn.softmax(s, axis=-1).astype(lat.dtype)
        return jnp.einsum('htk,kc->thc', p, lat)

    o_lat = lax.map(one_seq, (q_lat, q_pe, c_kv, k_pe, page_table))
    o = jnp.einsum('bthc,chv->bthv', o_lat, w_uv.reshape(KV_LORA, MLA_HEADS, V_DIM))
    return o.reshape(Bn, T, MLA_HEADS * V_DIM)


def moe_ffn(h, layer, w_group, b_group, w_expert, b_expert, w_gate, w_up, w_down):
    x = h.reshape(-1, h.shape[-1])
    T, D = x.shape
    lg = (x @ w_group[layer]).astype(F32) + b_group[layer].astype(F32)
    grp = jnp.argmax(lg, axis=-1).astype(jnp.int32)
    p_grp = jnp.take_along_axis(jax.nn.softmax(lg, axis=-1), grp[:, None], axis=1)
    le = ((x @ w_expert[layer]).astype(F32) + b_expert[layer].astype(F32)).reshape(T, N_GROUPS, EXP_PER_GROUP)
    le = jnp.take_along_axis(le, grp[:, None, None], axis=1)[:, 0]
    top_p, top_i = lax.top_k(jax.nn.softmax(le, axis=-1), TOP_K)
    gate = (p_grp * top_p / jnp.sum(top_p, axis=-1, keepdims=True)).reshape(-1)
    eid = (grp[:, None] * EXP_PER_GROUP + top_i.astype(jnp.int32)).reshape(-1)
    tok = jnp.repeat(jnp.arange(T, dtype=jnp.int32), TOP_K)
    A = T * TOP_K
    order = jnp.argsort(eid)
    e_s, tok_s, gate_s = eid[order], tok[order], gate[order]
    counts = jnp.zeros((N_EXPERTS,), jnp.int32).at[eid].add(1)
    padded = (counts + MOE_BLOCK - 1) // MOE_BLOCK * MOE_BLOCK
    pad_end = jnp.cumsum(padded)
    dest = (pad_end - padded)[e_s] + jnp.arange(A, dtype=jnp.int32) - (jnp.cumsum(counts) - counts)[e_s]
    n_blocks = -(-(A + N_EXPERTS * (MOE_BLOCK - 1)) // MOE_BLOCK)
    n_rows = n_blocks * MOE_BLOCK
    tok_buf = jnp.full((n_rows,), T, jnp.int32).at[dest].set(tok_s)
    gate_buf = jnp.zeros((n_rows,), F32).at[dest].set(gate_s)
    block_e = jnp.minimum(jnp.searchsorted(pad_end, jnp.arange(n_blocks, dtype=jnp.int32) * MOE_BLOCK,
                                           side='right'), N_EXPERTS - 1).astype(jnp.int32)
    x_pad = jnp.concatenate([x, jnp.zeros((1, D), x.dtype)], axis=0)
    xb = x_pad[tok_buf].reshape(n_blocks, MOE_BLOCK, D)

    def expert_block(args):
        xblk, e = args
        a = xblk @ w_gate[layer, e]
        u = xblk @ w_up[layer, e]
        return (jax.nn.silu(a) * u) @ w_down[layer, e]

    yb = lax.map(expert_block, (xb, block_e)).reshape(n_rows, D)
    y = jnp.zeros((T + 1, D), F32).at[tok_buf].add(yb.astype(F32) * gate_buf[:, None])[:T]
    return y.astype(h.dtype).reshape(h.shape)


def setup_inputs(seed: int = 0) -> dict:
    key = jax.random.key(seed)
    it = iter(jax.random.split(key, 32))

    def nrm(shape, scale=1.0):
        return jax.random.normal(next(it), shape, F32) * scale

    def gain(shape):
        return 1.0 + 0.02 * jax.random.normal(next(it), shape, F32)

    n_pages = PAST_LEN // PAGE_SIZE
    n_pool = (DEC_BATCH * n_pages * 5) // 4
    page_table = jax.random.permutation(next(it), n_pool)[:DEC_BATCH * n_pages]
    page_table = page_table.reshape(DEC_BATCH, n_pages).astype(jnp.int32)
    return {
        'x_prompt': nrm((BATCH, SEQ, D_MODEL)),
        'x_sample': nrm((DEC_BATCH, DEC_SEQ, D_MODEL)),
        'state_gla': nrm((N_MIX_LAYERS, DEC_BATCH, GLA_HEADS, GLA_DK, GLA_DV)),
        'state_conv': nrm((N_MIX_LAYERS, DEC_BATCH, CONV_WIDTH - 1, CONV_CH)),
        'cache_kv_latent': nrm((N_MLA_LAYERS, n_pool, PAGE_SIZE, KV_LORA)),
        'cache_k_rope': nrm((N_MLA_LAYERS, n_pool, PAGE_SIZE, ROPE)),
        'page_table': page_table,
        'g_mix_norm': gain((DEPTH, D_MODEL)),
        'g_ffn_norm': gain((DEPTH, D_MODEL)),
        'g_final_norm': gain((D_MODEL,)),
        'ab_w_in': nrm((N_MIX_LAYERS, D_MODEL, AB_IN_COLS), D_MODEL ** -0.5),
        'gla_w_gate2': nrm((N_MIX_LAYERS, GLA_GATE_RANK, GLA_KW), GLA_GATE_RANK ** -0.5),
        'gla_b_gate': nrm((N_MIX_LAYERS, GLA_KW), 0.1),
        'gla_g_out': gain((N_MIX_LAYERS, GLA_HEADS, GLA_DV)),
        'conv_w': nrm((N_MIX_LAYERS, CONV_WIDTH, CONV_CH), CONV_WIDTH ** -0.5),
        'ab_w_out': nrm((N_MIX_LAYERS, AB_MIX, D_MODEL), AB_MIX ** -0.5),
        'mla_w_in': nrm((N_MLA_LAYERS, D_MODEL, MLA_IN_COLS), D_MODEL ** -0.5),
        'mla_g_q': gain((N_MLA_LAYERS, Q_LORA)),
        'mla_g_kv': gain((N_MLA_LAYERS, KV_LORA)),
        'mla_w_uq': nrm((N_MLA_LAYERS, Q_LORA, MLA_HEADS * (NOPE + ROPE)), Q_LORA ** -0.5),
        'mla_w_uk': nrm((N_MLA_LAYERS, KV_LORA, MLA_HEADS * NOPE), KV_LORA ** -0.5),
        'mla_w_uv': nrm((N_MLA_LAYERS, KV_LORA, MLA_HEADS * V_DIM), KV_LORA ** -0.5),
        'mla_w_out': nrm((N_MLA_LAYERS, MLA_HEADS * V_DIM, D_MODEL), (MLA_HEADS * V_DIM) ** -0.5),
        'moe_w_group': nrm((DEPTH, D_MODEL, N_GROUPS), D_MODEL ** -0.5),
        'moe_b_group': nrm((DEPTH, N_GROUPS), 0.01),
        'moe_w_expert': nrm((DEPTH, D_MODEL, N_EXPERTS), D_MODEL ** -0.5),
        'moe_b_expert': nrm((DEPTH, N_EXPERTS), 0.01),
        'moe_w_gate': nrm((DEPTH, N_EXPERTS, D_MODEL, D_EXPERT), D_MODEL ** -0.5),
        'moe_w_up': nrm((DEPTH, N_EXPERTS, D_MODEL, D_EXPERT), D_MODEL ** -0.5),
        'moe_w_down': nrm((DEPTH, N_EXPERTS, D_EXPERT, D_MODEL), D_EXPERT ** -0.5),
    }


def reference(x_prompt, x_sample, state_gla, state_conv, cache_kv_latent, cache_k_rope, page_table,
              g_mix_norm, g_ffn_norm, g_final_norm,
              ab_w_in, gla_w_gate2, gla_b_gate, gla_g_out, conv_w, ab_w_out,
              mla_w_in, mla_g_q, mla_g_kv, mla_w_uq, mla_w_uk, mla_w_uv, mla_w_out,
              moe_w_group, moe_b_group, moe_w_expert, moe_b_expert, moe_w_gate, moe_w_up, moe_w_down):
    xp, xs = x_prompt, x_sample
    Bp, Sp = xp.shape[:2]
    Ts = xs.shape[1]
    past = page_table.shape[1] * PAGE_SIZE
    pos_p = jnp.arange(Sp)
    pos_s = past + jnp.arange(Ts)
    gla_p, conv_p, lat_p, pe_p = [], [], [], []
    gla_s, conv_s, lat_s, pe_s = [], [], [], []
    for layer in range(DEPTH):
        li = layer // 2
        hp = rmsnorm(xp, g_mix_norm[layer])
        hs = rmsnorm(xs, g_mix_norm[layer])
        if layer % 2 == 0:
            ab = (ab_w_in[li], gla_w_gate2[li], gla_b_gate[li], gla_g_out[li], conv_w[li], ab_w_out[li])
            zero_gla = jnp.zeros((Bp, GLA_HEADS, GLA_DK, GLA_DV), F32)
            zero_conv = jnp.zeros((Bp, CONV_WIDTH - 1, CONV_CH), xp.dtype)
            out_p, sg, sc = mixer_ab(hp, zero_gla, zero_conv, *ab)
            gla_p.append(sg)
            conv_p.append(sc)
            out_s, sg, sc = mixer_ab(hs, state_gla[li], state_conv[li], *ab)
            gla_s.append(sg)
            conv_s.append(sc)
        else:
            proj = (mla_w_in[li], mla_g_q[li], mla_g_kv[li], mla_w_uq[li])
            qn, qp, ckv, kpe = mla_project(hp, pos_p, *proj)
            out_p = mla_attend_prompt(qn, qp, ckv, kpe, mla_w_uk[li], mla_w_uv[li]) @ mla_w_out[li]
            lat_p.append(ckv)
            pe_p.append(kpe)
            qn, qp, ckv, kpe = mla_project(hs, pos_s, *proj)
            out_s = mla_attend_paged(qn, qp, ckv, kpe, mla_w_uk[li], mla_w_uv[li],
                                     cache_kv_latent, cache_k_rope, page_table, li) @ mla_w_out[li]
            lat_s.append(ckv)
            pe_s.append(kpe)
        xp = xp + out_p
        xs = xs + out_s
        moe = (moe_w_group, moe_b_group, moe_w_expert, moe_b_expert, moe_w_gate, moe_w_up, moe_w_down)
        xp = xp + moe_ffn(rmsnorm(xp, g_ffn_norm[layer]), layer, *moe)
        xs = xs + moe_ffn(rmsnorm(xs, g_ffn_norm[layer]), layer, *moe)
    y_prompt = rmsnorm(xp, g_final_norm)
    y_sample = rmsnorm(xs, g_final_norm)
    return (y_prompt, y_sample,
            jnp.stack(gla_p), jnp.stack(conv_p), jnp.stack(lat_p), jnp.stack(pe_p),
            jnp.stack(gla_s), jnp.stack(conv_s), jnp.stack(lat_s), jnp.stack(pe_s))
```

```python
import functools

import jax
import jax.numpy as jnp
from jax import lax
from jax.experimental import pallas as pl
from jax.experimental.pallas import tpu as pltpu

F32, BF16, I32 = jnp.float32, jnp.bfloat16, jnp.int32

D_MODEL = 4096
RMS_EPS = 1e-6
PAGE_SIZE = 128

GLA_HEADS = 4
GLA_VW = D_MODEL // 2
GLA_KW = GLA_VW // 2
GLA_DV = GLA_VW // GLA_HEADS
GLA_DK = GLA_KW // GLA_HEADS
GLA_GATE_RANK = 16
GLA_TAU = 16.0
GLA_CHUNK = 64
CONV_CH = D_MODEL // 2
CONV_WIDTH = 3

MLA_HEADS = D_MODEL // 128
Q_LORA = D_MODEL // 4
KV_LORA = 512
NOPE = 128
ROPE = 64
V_DIM = 128
MLA_SCALE = (NOPE + ROPE) ** -0.5
ROPE_THETA = 10000.0

N_GROUPS = 4
EXP_PER_GROUP = 8
N_EXPERTS = N_GROUPS * EXP_PER_GROUP
TOP_K = 2
D_EXPERT = D_MODEL // 4

LANES = 128
VMEM_LIMIT_BYTES = 56 * 1024 * 1024

COL_Q, COL_K, COL_V, COL_R = 0, GLA_KW, 2 * GLA_KW, 2 * GLA_KW + GLA_VW
COL_B = COL_R + GLA_VW
COL_C = COL_B + CONV_CH
COL_U = COL_C + CONV_CH
AB_MAIN_COLS = COL_U + CONV_CH

MOE_TM = 256
MOE_TF = 512
MOE_TN = 2048
GATHER_TB = 128
PAGES_PER_STEP = 16


def _params(n_axes):
    return pltpu.CompilerParams(dimension_semantics=("arbitrary",) * n_axes,
                                vmem_limit_bytes=VMEM_LIMIT_BYTES)


def _dot(a, b):
    return jnp.dot(a, b, preferred_element_type=F32)


def _dot_nt(a, b):
    return lax.dot_general(a, b, (((1,), (1,)), ((), ())), preferred_element_type=F32)


def _dot_tn(a, b):
    return lax.dot_general(a, b, (((0,), (0,)), ((), ())), preferred_element_type=F32)


def _rms(x, g):
    return x * lax.rsqrt(jnp.mean(x * x, axis=-1, keepdims=True) + RMS_EPS) * g


def _split3(x):
    hi = x.astype(BF16)
    r1 = x - hi.astype(F32)
    mid = r1.astype(BF16)
    lo = (r1 - mid.astype(F32)).astype(BF16)
    return hi, mid, lo


def _rmsnorm_body(x_ref, g_ref, o_ref):
    o_ref[...] = _rms(x_ref[...], g_ref[...]).astype(o_ref.dtype)


def rmsnorm(x, g, out_dtype, tm=256):
    t, d = x.shape
    return pl.pallas_call(
        _rmsnorm_body, grid=(t // tm,),
        in_specs=[pl.BlockSpec((tm, d), lambda i: (i, 0)), pl.BlockSpec((1, d), lambda i: (0, 0))],
        out_specs=pl.BlockSpec((tm, d), lambda i: (i, 0)),
        out_shape=jax.ShapeDtypeStruct((t, d), out_dtype),
        compiler_params=_params(1), name="rmsnorm")(x, g.reshape(1, d))


def _mm_body(*refs, has_res):
    a_ref, w_ref = refs[0], refs[1]
    if has_res:
        res_ref, o_ref, wb_ref = refs[2:]
    else:
        o_ref, wb_ref = refs[2:]

    @pl.when(pl.program_id(1) == 0)
    def _cast_weights():
        wb_ref[...] = w_ref[...].astype(BF16)

    acc = _dot(a_ref[...], wb_ref[...])
    if has_res:
        acc = acc + res_ref[...]
    o_ref[...] = acc.astype(o_ref.dtype)


def matmul_ws(a, w, lead, col0, n_cols, tm, tn, out_dtype, res=None):
    m, k = a.shape
    cb0 = col0 // tn
    wspec = pl.BlockSpec((None,) * len(lead) + (k, tn), lambda j, i: tuple(lead) + (0, cb0 + j))
    in_specs = [pl.BlockSpec((tm, k), lambda j, i: (i, 0)), wspec]
    args = [a, w]
    if res is not None:
        in_specs.append(pl.BlockSpec((tm, tn), lambda j, i: (i, j)))
        args.append(res)
    return pl.pallas_call(
        functools.partial(_mm_body, has_res=res is not None),
        grid=(n_cols // tn, m // tm), in_specs=in_specs,
        out_specs=pl.BlockSpec((tm, tn), lambda j, i: (i, j)),
        out_shape=jax.ShapeDtypeStruct((m, n_cols), out_dtype),
        scratch_shapes=[pltpu.VMEM((k, tn), BF16)],
        compiler_params=_params(2), name="matmul_ws")(*args)


def _log_sigmoid(x):
    return jnp.minimum(x, 0.0) - jnp.log1p(jnp.exp(-jnp.abs(x)))


def _gla_body(*refs, chunk, has_s0):
    q_ref, k_ref, v_ref, r_ref, g_ref, w2_ref, bg_ref, go_ref = refs[:8]
    if has_s0:
        s0_ref, o_ref, s_ref = refs[8:]
    else:
        o_ref, s_ref = refs[8:]

    @pl.when(pl.program_id(2) == 0)
    def _init_state():
        s_ref[...] = s0_ref[...] if has_s0 else jnp.zeros_like(s_ref)

    q = q_ref[...] * (GLA_DK ** -0.5)
    k = k_ref[...]
    v = v_ref[...].astype(BF16)
    gate_in = _dot(g_ref[...].astype(BF16), w2_ref[...].astype(BF16)) + bg_ref[...]
    log_a = _log_sigmoid(gate_in) / GLA_TAU

    row = lax.broadcasted_iota(I32, (chunk, chunk), 0)
    col = lax.broadcasted_iota(I32, (chunk, chunk), 1)
    causal = row >= col
    tri = jnp.where(causal, 1.0, 0.0).astype(BF16)
    ones = jnp.ones((chunk, GLA_DV), BF16)
    parts = _split3(log_a)
    b = _dot(tri, parts[0]) + _dot(tri, parts[1]) + _dot(tri, parts[2])
    b_tot_col = _dot_tn(parts[0], ones) + _dot_tn(parts[1], ones) + _dot_tn(parts[2], ones)
    b_last = b[chunk - 1:chunk, :]

    q_dec = (q * jnp.exp(b)).astype(BF16)
    k_inv = (k * jnp.exp(-b)).astype(BF16)
    k_end = (k * jnp.exp(b_last - b)).astype(BF16)
    att = jnp.where(causal, _dot_nt(q_dec, k_inv), 0.0).astype(BF16)
    s_old = s_ref[...]
    o = _dot(q_dec, s_old.astype(BF16)) + _dot(att, v)
    s_ref[...] = jnp.exp(b_tot_col) * s_old + _dot_tn(k_end, v)

    r = r_ref[...]
    o_ref[...] = (_rms(o, go_ref[...]) * (r * jax.nn.sigmoid(r))).astype(o_ref.dtype)


def gla(proj, gate_lr, w2_pad, b_gate, g_out, s0, li, n_seq, seq_len, row0, chunk):
    nc = seq_len // chunk
    rb0 = row0 // chunk

    def rows(b, h, c):
        return rb0 + b * nc + c

    in_specs = [
        pl.BlockSpec((chunk, GLA_DK), lambda b, h, c: (rows(b, h, c), COL_Q // GLA_DK + h)),
        pl.BlockSpec((chunk, GLA_DK), lambda b, h, c: (rows(b, h, c), COL_K // GLA_DK + h)),
        pl.BlockSpec((chunk, GLA_DV), lambda b, h, c: (rows(b, h, c), COL_V // GLA_DV + h)),
        pl.BlockSpec((chunk, GLA_DV), lambda b, h, c: (rows(b, h, c), COL_R // GLA_DV + h)),
        pl.BlockSpec((chunk, LANES), lambda b, h, c: (rows(b, h, c), 0)),
        pl.BlockSpec((LANES, GLA_DK), lambda b, h, c: (0, h)),
        pl.BlockSpec((1, GLA_DK), lambda b, h, c: (0, h)),
        pl.BlockSpec((1, GLA_DV), lambda b, h, c: (0, h)),
    ]
    args = [proj, proj, proj, proj, gate_lr, w2_pad, b_gate, g_out]
    if s0 is not None:
        in_specs.append(pl.BlockSpec((None, None, None, GLA_DK, GLA_DV), lambda b, h, c: (li, b, h, 0, 0)))
        args.append(s0)
    return pl.pallas_call(
        functools.partial(_gla_body, chunk=chunk, has_s0=s0 is not None),
        grid=(n_seq, GLA_HEADS, nc), in_specs=in_specs,
        out_specs=[pl.BlockSpec((chunk, GLA_DV), lambda b, h, c: (b * nc + c, h)),
                   pl.BlockSpec((None, None, None, GLA_DK, GLA_DV), lambda b, h, c: (0, b, h, 0, 0))],
        out_shape=[jax.ShapeDtypeStruct((n_seq * seq_len, GLA_VW), F32),
                   jax.ShapeDtypeStruct((1, n_seq, GLA_HEADS, GLA_DK, GLA_DV), F32)],
        compiler_params=_params(3), name="gla")(*args)


def _conv_body(gb_ref, gc_ref, u_ref, buf_ref, w_ref, o_ref, st_ref, sc_ref, *, seq_len):
    cu = gc_ref[...] * u_ref[...]
    sc_ref[0:8, :] = jnp.zeros((8, cu.shape[1]), F32)
    sc_ref[6:8, :] = buf_ref[...]
    sc_ref[8:8 + seq_len, :] = cu
    w = w_ref[...]
    z = sc_ref[6:6 + seq_len, :] * w[0:1, :] + sc_ref[7:7 + seq_len, :] * w[1:2, :] + cu * w[2:3, :]
    o_ref[...] = gb_ref[...] * z
    st_ref[...] = sc_ref[6 + seq_len:8 + seq_len, :]


def short_conv(proj, buf, conv_w, li_buf, li_w, n_seq, seq_len, row0, ct):
    rb0 = row0 // seq_len
    return pl.pallas_call(
        functools.partial(_conv_body, seq_len=seq_len),
        grid=(n_seq, CONV_CH // ct),
        in_specs=[
            pl.BlockSpec((seq_len, ct), lambda b, j: (rb0 + b, COL_B // ct + j)),
            pl.BlockSpec((seq_len, ct), lambda b, j: (rb0 + b, COL_C // ct + j)),
            pl.BlockSpec((seq_len, ct), lambda b, j: (rb0 + b, COL_U // ct + j)),
            pl.BlockSpec((None, None, CONV_WIDTH - 1, ct), lambda b, j: (li_buf, b, 0, j)),
            pl.BlockSpec((None, CONV_WIDTH, ct), lambda b, j: (li_w, 0, j)),
        ],
        out_specs=[pl.BlockSpec((seq_len, ct), lambda b, j: (b, j)),
                   pl.BlockSpec((None, None, CONV_WIDTH - 1, ct), lambda b, j: (0, b, 0, j))],
        out_shape=[jax.ShapeDtypeStruct((n_seq * seq_len, CONV_CH), F32),
                   jax.ShapeDtypeStruct((1, n_seq, CONV_WIDTH - 1, CONV_CH), F32)],
        scratch_shapes=[pltpu.VMEM((seq_len + 8, ct), F32)],
        compiler_params=_params(2), name="short_conv")(proj, proj, proj, buf, conv_w)


def _swap_halves(x, lane):
    return jnp.where(lane % ROPE < ROPE // 2, pltpu.roll(x, LANES - ROPE // 2, 1), pltpu.roll(x, ROPE // 2, 1))


def _mla_post_body(cq_ref, ckv_ref, kpe_ref, cos_ref, sin_ref, gq_ref, gkv_ref,
                   cqn_ref, lat_ref, latb_ref, pe_ref, peb_ref):
    cqn_ref[...] = _rms(cq_ref[...], gq_ref[...]).astype(BF16)
    lat = _rms(ckv_ref[...], gkv_ref[...])
    lat_ref[...] = lat
    latb_ref[...] = lat.astype(BF16)
    kpe = kpe_ref[...]
    lane = lax.broadcasted_iota(I32, kpe.shape, 1)
    pe = kpe * cos_ref[...] + _swap_halves(kpe, lane) * sin_ref[...]
    pe_ref[...] = pe
    peb_ref[...] = pe.astype(BF16)


def mla_post(c, kpe_raw, cos, sin, g_q, g_kv, tm=256):
    t = c.shape[0]
    return pl.pallas_call(
        _mla_post_body, grid=(t // tm,),
        in_specs=[
            pl.BlockSpec((tm, Q_LORA), lambda i: (i, 0)),
            pl.BlockSpec((tm, KV_LORA), lambda i: (i, Q_LORA // KV_LORA)),
            pl.BlockSpec((tm, LANES), lambda i: (i, 0)),
            pl.BlockSpec((tm, LANES), lambda i: (i, 0)),
            pl.BlockSpec((tm, LANES), lambda i: (i, 0)),
            pl.BlockSpec((1, Q_LORA), lambda i: (0, 0)),
            pl.BlockSpec((1, KV_LORA), lambda i: (0, 0)),
        ],
        out_specs=[
            pl.BlockSpec((tm, Q_LORA), lambda i: (i, 0)),
            pl.BlockSpec((tm, KV_LORA), lambda i: (i, 0)),
            pl.BlockSpec((tm, KV_LORA), lambda i: (i, 0)),
            pl.BlockSpec((tm, LANES), lambda i: (i, 0)),
            pl.BlockSpec((tm, LANES), lambda i: (i, 0)),
        ],
        out_shape=[
            jax.ShapeDtypeStruct((t, Q_LORA), BF16),
            jax.ShapeDtypeStruct((t, KV_LORA), F32),
            jax.ShapeDtypeStruct((t, KV_LORA), BF16),
            jax.ShapeDtypeStruct((t, LANES), F32),
            jax.ShapeDtypeStruct((t, LANES), BF16),
        ],
        compiler_params=_params(1), name="mla_post")(c, c, kpe_raw, cos, sin, g_q, g_kv)


def _q_post_body(q_ref, cos_ref, sin_ref, o_ref, *, n_nope_tiles):
    j = pl.program_id(1)

    @pl.when(j < n_nope_tiles)
    def _nope():
        o_ref[...] = (q_ref[...] * MLA_SCALE).astype(BF16)

    @pl.when(j >= n_nope_tiles)
    def _rope():
        cos, sin = cos_ref[...], sin_ref[...]
        lane = lax.broadcasted_iota(I32, cos.shape, 1)
        for c in range(q_ref.shape[1] // LANES):
            x = q_ref[:, c * LANES:(c + 1) * LANES]
            y = x * cos + _swap_halves(x, lane) * sin
            o_ref[:, c * LANES:(c + 1) * LANES] = (y * MLA_SCALE).astype(BF16)


def q_post(q, cos, sin, tm=256, tn=1024):
    t, n = q.shape
    n_nope_tiles = MLA_HEADS * NOPE // tn
    return pl.pallas_call(
        functools.partial(_q_post_body, n_nope_tiles=n_nope_tiles), grid=(t // tm, n // tn),
        in_specs=[pl.BlockSpec((tm, tn), lambda i, j: (i, j)),
                  pl.BlockSpec((tm, LANES), lambda i, j: (i, 0)),
                  pl.BlockSpec((tm, LANES), lambda i, j: (i, 0))],
        out_specs=pl.BlockSpec((tm, tn), lambda i, j: (i, j)),
        out_shape=jax.ShapeDtypeStruct((t, n), BF16),
        compiler_params=_params(2), name="q_post")(q, cos, sin)


def _flash_body(qn_ref, qp_ref, kn_ref, kp_ref, v_ref, o_ref, *, tq, tk):
    qi = pl.program_id(2)
    kp_all = kp_ref
    for hh in range(2):
        hs = slice(hh * NOPE, (hh + 1) * NOPE)
        qn = qn_ref[:, hs]
        qp = qp_ref[:, hs]
        q_pos = qi * tq + lax.broadcasted_iota(I32, (tq, tk), 0)

        def step(j, carry):
            m, l, acc = carry
            ks = pl.ds(pl.multiple_of(j * tk, tk), tk)
            s = _dot_nt(qn, kn_ref[ks, hs]) + _dot_nt(qp, kp_all[ks, :])
            k_pos = j * tk + lax.broadcasted_iota(I32, (tq, tk), 1)
            s = jnp.where(q_pos >= k_pos, s, -jnp.inf)
            m_new = jnp.maximum(m, jnp.max(s, axis=1, keepdims=True))
            alpha = jnp.exp(m - m_new)
            p = jnp.exp(s - m_new)
            l = alpha * l + jnp.sum(p, axis=1, keepdims=True)
            acc = alpha * acc + _dot(p.astype(BF16), v_ref[ks, hs])
            return m_new, l, acc

        n_kv = (qi * tq + tq + tk - 1) // tk
        init = (jnp.full((tq, 1), -jnp.inf, F32), jnp.zeros((tq, 1), F32), jnp.zeros((tq, V_DIM), F32))
        m, l, acc = lax.fori_loop(0, n_kv, step, init)
        o_ref[:, hs] = (acc / l).astype(o_ref.dtype)


def flash_prompt(qs, kn, kp, v, n_seq, seq_len, tq=256, tk=256):
    nq = seq_len // tq
    hw = 2 * NOPE
    rope_cb0 = MLA_HEADS * NOPE // hw
    return pl.pallas_call(
        functools.partial(_flash_body, tq=tq, tk=tk),
        grid=(n_seq, MLA_HEADS // 2, nq),
        in_specs=[
            pl.BlockSpec((tq, hw), lambda b, h, i: (b * nq + i, h)),
            pl.BlockSpec((tq, hw), lambda b, h, i: (b * nq + i, rope_cb0 + h)),
            pl.BlockSpec((seq_len, hw), lambda b, h, i: (b, h)),
            pl.BlockSpec((seq_len, LANES), lambda b, h, i: (b, 0)),
            pl.BlockSpec((seq_len, hw), lambda b, h, i: (b, h)),
        ],
        out_specs=pl.BlockSpec((tq, hw), lambda b, h, i: (b * nq + i, h)),
        out_shape=jax.ShapeDtypeStruct((n_seq * seq_len, MLA_HEADS * V_DIM), BF16),
        compiler_params=_params(3), name="flash_prompt")(qs, qs, kn, kp, v)


def _head_proj_body(a_ref, w_ref, o_ref, *, transposed):
    w = w_ref[...].astype(BF16)
    a = a_ref[...]
    o_ref[...] = (_dot_nt(a, w) if transposed else _dot(a, w)).astype(o_ref.dtype)


def absorb_q(qs, w_uk, li, row0, n_rows):
    return pl.pallas_call(
        functools.partial(_head_proj_body, transposed=True), grid=(MLA_HEADS,),
        in_specs=[pl.BlockSpec((n_rows, NOPE), lambda h: (row0 // n_rows, h)),
                  pl.BlockSpec((None, KV_LORA, NOPE), lambda h: (li, 0, h))],
        out_specs=pl.BlockSpec((n_rows, KV_LORA), lambda h: (0, h)),
        out_shape=jax.ShapeDtypeStruct((n_rows, MLA_HEADS * KV_LORA), BF16),
        compiler_params=_params(1), name="absorb_q")(qs, w_uk)


def expand_v(o_lat, w_uv, li):
    n_rows = o_lat.shape[0]
    return pl.pallas_call(
        functools.partial(_head_proj_body, transposed=False), grid=(MLA_HEADS,),
        in_specs=[pl.BlockSpec((n_rows, KV_LORA), lambda h: (0, h)),
                  pl.BlockSpec((None, KV_LORA, V_DIM), lambda h: (li, 0, h))],
        out_specs=pl.BlockSpec((n_rows, V_DIM), lambda h: (0, h)),
        out_shape=jax.ShapeDtypeStruct((n_rows, MLA_HEADS * V_DIM), BF16),
        compiler_params=_params(1), name="expand_v")(o_lat, w_uv)


def _decode_body(pt_ref, ql_ref, qp_ref, latn_ref, pen_ref, lat_hbm, pe_hbm, o_ref,
                 lat_buf, pe_buf, sem, m_ref, l_ref, acc_ref, *, li, n_pages, dec_seq):
    s_idx = pl.program_id(1)
    n_steps = pl.num_programs(1)
    step = pl.program_id(0) * n_steps + s_idx
    n_rows = ql_ref.shape[0]

    def page_copies(seq, kv_step, slot):
        out = []
        for kk in range(n_pages):
            page = pt_ref[seq, kv_step * n_pages + kk]
            rows = pl.ds(kk * PAGE_SIZE, PAGE_SIZE)
            out.append(pltpu.make_async_copy(lat_hbm.at[li, page], lat_buf.at[slot, rows], sem.at[0, slot]))
            out.append(pltpu.make_async_copy(pe_hbm.at[li, page], pe_buf.at[slot, rows], sem.at[1, slot]))
        return out

    @pl.when(step == 0)
    def _first():
        for cp in page_copies(0, 0, 0):
            cp.start()

    @pl.when(step + 1 < pl.num_programs(0) * n_steps)
    def _prefetch():
        nxt = step + 1
        for cp in page_copies(nxt // n_steps, nxt % n_steps, nxt % 2):
            cp.start()

    @pl.when(s_idx == 0)
    def _init():
        m_ref[...] = jnp.full(m_ref.shape, -jnp.inf, F32)
        l_ref[...] = jnp.zeros(l_ref.shape, F32)
        acc_ref[...] = jnp.zeros(acc_ref.shape, F32)

    ql = ql_ref[...]
    qp = qp_ref[:, :ROPE]

    def update(s, keys):
        m_prev = m_ref[...]
        m_new = jnp.maximum(m_prev, jnp.max(s, axis=1, keepdims=True))
        alpha = jnp.exp(m_prev - m_new)
        p = jnp.exp(s - m_new)
        l_ref[...] = alpha * l_ref[...] + jnp.sum(p, axis=1, keepdims=True)
        acc_ref[...] = alpha * acc_ref[...] + _dot(p.astype(BF16), keys)
        m_ref[...] = m_new

    slot = step % 2
    for cp in page_copies(pl.program_id(0), s_idx, slot):
        cp.wait()
    keys = lat_buf[slot].astype(BF16)
    update(_dot_nt(ql, keys) + _dot_nt(qp, pe_buf[slot].astype(BF16)), keys)

    @pl.when(s_idx == n_steps - 1)
    def _new_tokens():
        pad = PAGE_SIZE - dec_seq
        lat_new = jnp.concatenate([latn_ref[...], jnp.zeros((pad, KV_LORA), F32)], axis=0).astype(BF16)
        pe_new = jnp.concatenate([pen_ref[:, :ROPE], jnp.zeros((pad, ROPE), F32)], axis=0).astype(BF16)
        s = _dot_nt(ql, lat_new) + _dot_nt(qp, pe_new)
        q_tok = lax.broadcasted_iota(I32, (n_rows, PAGE_SIZE), 0) // MLA_HEADS
        key_tok = lax.broadcasted_iota(I32, (n_rows, PAGE_SIZE), 1)
        update(jnp.where(key_tok <= q_tok, s, -jnp.inf), lat_new)
        o_ref[...] = (acc_ref[...] / l_ref[...]).astype(o_ref.dtype)


def decode_attention(page_table, q_lat, q_pe, cache_lat, cache_pe, lat_new, pe_new, li, row0, dec_seq):
    n_seq, n_rows, _ = q_lat.shape
    n_steps = page_table.shape[1] // PAGES_PER_STEP
    n_keys = PAGES_PER_STEP * PAGE_SIZE
    grid_spec = pltpu.PrefetchScalarGridSpec(
        num_scalar_prefetch=1, grid=(n_seq, n_steps),
        in_specs=[pl.BlockSpec((None, n_rows, KV_LORA), lambda b, s, pt: (b, 0, 0)),
                  pl.BlockSpec((None, n_rows, LANES), lambda b, s, pt: (b, 0, 0)),
                  pl.BlockSpec((dec_seq, KV_LORA), lambda b, s, pt: (row0 // dec_seq + b, 0)),
                  pl.BlockSpec((dec_seq, LANES), lambda b, s, pt: (row0 // dec_seq + b, 0)),
                  pl.BlockSpec(memory_space=pl.ANY), pl.BlockSpec(memory_space=pl.ANY)],
        out_specs=pl.BlockSpec((None, n_rows, KV_LORA), lambda b, s, pt: (b, 0, 0)),
        scratch_shapes=[pltpu.VMEM((2, n_keys, KV_LORA), F32), pltpu.VMEM((2, n_keys, ROPE), F32),
                        pltpu.SemaphoreType.DMA((2, 2)),
                        pltpu.VMEM((n_rows, 1), F32), pltpu.VMEM((n_rows, 1), F32),
                        pltpu.VMEM((n_rows, KV_LORA), F32)])
    return pl.pallas_call(
        functools.partial(_decode_body, li=li, n_pages=PAGES_PER_STEP, dec_seq=dec_seq),
        grid_spec=grid_spec, out_shape=jax.ShapeDtypeStruct((n_seq, n_rows, KV_LORA), BF16),
        compiler_params=_params(2), name="decode_attention")(
            page_table, q_lat, q_pe, lat_new, pe_new, cache_lat, cache_pe)


def _router_body(x_ref, g_ref, w_ref, b_ref, o_ref):
    h = _rms(x_ref[...], g_ref[...])
    logits = jnp.dot(h, w_ref[...], preferred_element_type=F32, precision=lax.Precision.HIGHEST) + b_ref[...]
    lane = lax.broadcasted_iota(I32, logits.shape, 1)
    lane_f = lane.astype(F32)
    neg = -jnp.inf
    lg = jnp.where(lane < N_GROUPS, logits, neg)
    mg = jnp.max(lg, axis=1, keepdims=True)
    grp = jnp.min(jnp.where(lg == mg, lane_f, float(LANES)), axis=1, keepdims=True)
    p_grp = 1.0 / jnp.sum(jnp.exp(lg - mg), axis=1, keepdims=True)
    e_lane = lane_f - float(N_GROUPS)
    in_grp = (lane >= N_GROUPS) & (lane < N_GROUPS + N_EXPERTS) & (jnp.floor(e_lane / EXP_PER_GROUP) == grp)
    le = jnp.where(in_grp, logits, neg)
    m1 = jnp.max(le, axis=1, keepdims=True)
    i1 = jnp.min(jnp.where(le == m1, lane_f, float(LANES)), axis=1, keepdims=True)
    le2 = jnp.where(lane_f == i1, neg, le)
    m2 = jnp.max(le2, axis=1, keepdims=True)
    i2 = jnp.min(jnp.where(le2 == m2, lane_f, float(LANES)), axis=1, keepdims=True)
    e2 = jnp.exp(m2 - m1)
    g1 = p_grp / (1.0 + e2)
    g2 = p_grp * e2 / (1.0 + e2)
    out = jnp.where(lane == 0, i1 - N_GROUPS, jnp.where(lane == 1, i2 - N_GROUPS,
                    jnp.where(lane == 2, g1, jnp.where(lane == 3, g2, 0.0))))
    o_ref[...] = out


def moe_router(x, g, w_router, b_router, tm=256):
    t, d = x.shape
    return pl.pallas_call(
        _router_body, grid=(t // tm,),
        in_specs=[pl.BlockSpec((tm, d), lambda i: (i, 0)), pl.BlockSpec((1, d), lambda i: (0, 0)),
                  pl.BlockSpec((d, LANES), lambda i: (0, 0)), pl.BlockSpec((1, LANES), lambda i: (0, 0))],
        out_specs=pl.BlockSpec((tm, LANES), lambda i: (i, 0)),
        out_shape=jax.ShapeDtypeStruct((t, LANES), F32),
        compiler_params=_params(1), name="moe_router")(x, g.reshape(1, d), w_router, b_router)


def _row_copy(src_hbm, dst, sem, src_row, dst_row):
    return pltpu.make_async_copy(src_hbm.at[pl.ds(src_row, 1)], dst.at[pl.ds(dst_row, 1)], sem)


def _dispatch_body(tok_ref, x_hbm, g_ref, o_ref, buf, sem, *, tb):
    i = pl.program_id(0)
    n = pl.num_programs(0)

    def start_all(step, slot):
        def body(r, c):
            _row_copy(x_hbm, buf.at[slot], sem.at[slot], tok_ref[step * tb + r], r).start()
            return c
        lax.fori_loop(0, tb, body, 0)

    def wait_all(slot):
        def body(r, c):
            _row_copy(x_hbm, buf.at[slot], sem.at[slot], 0, r).wait()
            return c
        lax.fori_loop(0, tb, body, 0)

    @pl.when(i == 0)
    def _first():
        start_all(0, 0)

    @pl.when(i + 1 < n)
    def _prefetch():
        start_all(i + 1, (i + 1) % 2)

    slot = i % 2
    wait_all(slot)
    o_ref[...] = _rms(buf[slot], g_ref[...]).astype(o_ref.dtype)


def moe_dispatch(tok_sorted, x, g, tb=GATHER_TB):
    n_rows = tok_sorted.shape[0]
    d = x.shape[1]
    grid_spec = pltpu.PrefetchScalarGridSpec(
        num_scalar_prefetch=1, grid=(n_rows // tb,),
        in_specs=[pl.BlockSpec(memory_space=pl.ANY), pl.BlockSpec((1, d), lambda i, tok: (0, 0))],
        out_specs=pl.BlockSpec((tb, d), lambda i, tok: (i, 0)),
        scratch_shapes=[pltpu.VMEM((2, tb, d), F32), pltpu.SemaphoreType.DMA((2,))])
    return pl.pallas_call(
        functools.partial(_dispatch_body, tb=tb), grid_spec=grid_spec,
        out_shape=jax.ShapeDtypeStruct((n_rows, d), BF16),
        compiler_params=_params(1), name="moe_dispatch")(tok_sorted, x, g.reshape(1, d))


def _moe_up_body(e_ref, cw_ref, rb_ref, co_ref, first_ref, valid_ref, x_ref, wg_ref, wu_ref, o_ref, wgb, wub):
    t = pl.program_id(0)

    @pl.when(first_ref[t] == 1)
    def _cast_weights():
        wgb[...] = wg_ref[...].astype(BF16)
        wub[...] = wu_ref[...].astype(BF16)

    @pl.when(valid_ref[t] == 1)
    def _compute():
        x = x_ref[...]
        a = _dot(x, wgb[...])
        u = _dot(x, wub[...])
        o_ref[...] = (a * jax.nn.sigmoid(a) * u).astype(o_ref.dtype)

    @pl.when(valid_ref[t] == 0)
    def _unused_block():
        o_ref[...] = jnp.zeros_like(o_ref)


def moe_up(items, xs, w_gate, w_up, layer):
    n_rows, d = xs.shape
    n_items = items[0].shape[0]
    wspec = pl.BlockSpec((None, None, d, MOE_TF), lambda t, e, cw, rb, co, fi, va: (layer, e[t], 0, cw[t]))
    grid_spec = pltpu.PrefetchScalarGridSpec(
        num_scalar_prefetch=6, grid=(n_items,),
        in_specs=[pl.BlockSpec((MOE_TM, d), lambda t, e, cw, rb, co, fi, va: (rb[t], 0)), wspec, wspec],
        out_specs=pl.BlockSpec((MOE_TM, MOE_TF), lambda t, e, cw, rb, co, fi, va: (rb[t], co[t])),
        scratch_shapes=[pltpu.VMEM((d, MOE_TF), BF16), pltpu.VMEM((d, MOE_TF), BF16)])
    return pl.pallas_call(
        _moe_up_body, grid_spec=grid_spec, out_shape=jax.ShapeDtypeStruct((n_rows, D_EXPERT), BF16),
        compiler_params=_params(1), name="moe_up")(*items, xs, w_gate, w_up)


def _moe_down_body(e_ref, cw_ref, rb_ref, co_ref, first_ref, valid_ref, h_ref, wd_ref, o_ref, wdb):
    t = pl.program_id(0)

    @pl.when(first_ref[t] == 1)
    def _cast_weights():
        wdb[...] = wd_ref[...].astype(BF16)

    @pl.when(valid_ref[t] == 1)
    def _compute():
        o_ref[...] = _dot(h_ref[...], wdb[...]).astype(o_ref.dtype)

    @pl.when(valid_ref[t] == 0)
    def _unused_block():
        o_ref[...] = jnp.zeros_like(o_ref)


def moe_down(items, hmid, w_down, layer):
    n_rows, f = hmid.shape
    d = w_down.shape[-1]
    n_items = items[0].shape[0]
    grid_spec = pltpu.PrefetchScalarGridSpec(
        num_scalar_prefetch=6, grid=(n_items,),
        in_specs=[pl.BlockSpec((MOE_TM, f), lambda t, e, cw, rb, co, fi, va: (rb[t], 0)),
                  pl.BlockSpec((None, None, f, MOE_TN), lambda t, e, cw, rb, co, fi, va: (layer, e[t], 0, cw[t]))],
        out_specs=pl.BlockSpec((MOE_TM, MOE_TN), lambda t, e, cw, rb, co, fi, va: (rb[t], co[t])),
        scratch_shapes=[pltpu.VMEM((f, MOE_TN), BF16)])
    return pl.pallas_call(
        _moe_down_body, grid_spec=grid_spec, out_shape=jax.ShapeDtypeStruct((n_rows, d), F32),
        compiler_params=_params(1), name="moe_down")(*items, hmid, w_down)


def _combine_body(pos_ref, x_ref, gate_ref, gn_ref, yb_hbm, o_ref, buf, sem, *, tb, final_norm):
    i = pl.program_id(0)
    n = pl.num_programs(0)

    def start_all(step, slot):
        def body(r, c):
            for kk in range(TOP_K):
                _row_copy(yb_hbm, buf.at[slot, kk], sem.at[slot], pos_ref[(step * tb + r) * TOP_K + kk], r).start()
            return c
        lax.fori_loop(0, tb, body, 0)

    def wait_all(slot):
        def body(r, c):
            for kk in range(TOP_K):
                _row_copy(yb_hbm, buf.at[slot, kk], sem.at[slot], 0, r).wait()
            return c
        lax.fori_loop(0, tb, body, 0)

    @pl.when(i == 0)
    def _first():
        start_all(0, 0)

    @pl.when(i + 1 < n)
    def _prefetch():
        start_all(i + 1, (i + 1) % 2)

    slot = i % 2
    wait_all(slot)
    gate = gate_ref[...]
    y = x_ref[...] + (gate[:, 0:1] * buf[slot, 0] + gate[:, 1:2] * buf[slot, 1])
    if final_norm:
        y = _rms(y, gn_ref[...])
    o_ref[...] = y


def moe_combine(pos, x, gates, yb, g_norm, final_norm, tb=GATHER_TB):
    t, d = x.shape
    grid_spec = pltpu.PrefetchScalarGridSpec(
        num_scalar_prefetch=1, grid=(t // tb,),
        in_specs=[pl.BlockSpec((tb, d), lambda i, p: (i, 0)), pl.BlockSpec((tb, TOP_K), lambda i, p: (i, 0)),
                  pl.BlockSpec((1, d), lambda i, p: (0, 0)), pl.BlockSpec(memory_space=pl.ANY)],
        out_specs=pl.BlockSpec((tb, d), lambda i, p: (i, 0)),
        scratch_shapes=[pltpu.VMEM((2, TOP_K, tb, d), F32), pltpu.SemaphoreType.DMA((2,))])
    return pl.pallas_call(
        functools.partial(_combine_body, tb=tb, final_norm=final_norm), grid_spec=grid_spec,
        out_shape=jax.ShapeDtypeStruct((t, d), F32),
        compiler_params=_params(1), name="moe_combine")(pos, x, gates, g_norm.reshape(1, d), yb)


def _work_items(n_col_tiles, blocks_per_expert, block_start, n_used, n_blocks):
    n_items = n_col_tiles * n_blocks
    idx = jnp.arange(n_items, dtype=I32)
    n_valid = n_col_tiles * n_used
    valid = idx < n_valid
    t = jnp.minimum(idx, n_valid - 1)
    item_end = n_col_tiles * jnp.cumsum(blocks_per_expert)
    e = jnp.minimum(jnp.searchsorted(item_end, t, side='right'), N_EXPERTS - 1).astype(I32)
    nb = jnp.maximum(blocks_per_expert[e], 1)
    local = t - n_col_tiles * block_start[e]
    cw = (local // nb).astype(I32)
    r = local % nb
    spare = idx - n_valid
    rb = jnp.where(valid, block_start[e] + r, n_used + spare // n_col_tiles).astype(I32)
    co = jnp.where(valid, cw, spare % n_col_tiles).astype(I32)
    first = ((r == 0) & valid).astype(I32)
    return e, cw, rb, co, first, valid.astype(I32)


def moe_layer(x, layer, g_norm, w_group, b_group, w_expert, b_expert, w_gate, w_up, w_down, g_final, final_norm):
    t, d = x.shape
    pad = LANES - N_GROUPS - N_EXPERTS
    w_router = jnp.concatenate([w_group[layer], w_expert[layer], jnp.zeros((d, pad), F32)], axis=1)
    b_router = jnp.concatenate([b_group[layer], b_expert[layer], jnp.zeros((pad,), F32)]).reshape(1, LANES)
    routed = moe_router(x, g_norm, w_router, b_router)
    eid = routed[:, 0:TOP_K].astype(I32).reshape(-1)
    gates = routed[:, TOP_K:2 * TOP_K]

    n_assign = t * TOP_K
    onehot = (eid[:, None] == jnp.arange(N_EXPERTS, dtype=I32)[None, :]).astype(I32)
    rank = jnp.sum((jnp.cumsum(onehot, axis=0) - onehot) * onehot, axis=1)
    counts = jnp.sum(onehot, axis=0)
    blocks_per_expert = (counts + MOE_TM - 1) // MOE_TM
    block_start = jnp.cumsum(blocks_per_expert) - blocks_per_expert
    n_used = jnp.sum(blocks_per_expert)
    n_blocks = -(-(n_assign + N_EXPERTS * (MOE_TM - 1)) // MOE_TM)
    pos = (block_start[eid] * MOE_TM + rank).astype(I32)
    tok = jnp.arange(n_assign, dtype=I32) // TOP_K
    tok_sorted = jnp.zeros((n_blocks * MOE_TM,), I32).at[pos].set(tok)

    xs = moe_dispatch(tok_sorted, x, g_norm)
    hmid = moe_up(_work_items(D_EXPERT // MOE_TF, blocks_per_expert, block_start, n_used, n_blocks),
                  xs, w_gate, w_up, layer)
    yb = moe_down(_work_items(d // MOE_TN, blocks_per_expert, block_start, n_used, n_blocks),
                  hmid, w_down, layer)
    return moe_combine(pos, x, gates, yb, g_final, final_norm)


def _rope_tables(pos):
    inv = ROPE_THETA ** (-jnp.arange(0, ROPE, 2, dtype=F32) / ROPE)
    ang = pos.astype(F32)[:, None] * inv[None, :]
    cos, sin = jnp.cos(ang), jnp.sin(ang)
    return jnp.concatenate([cos] * 4, axis=1), jnp.concatenate([-sin, sin, -sin, sin], axis=1)


def kernel(x_prompt, x_sample, state_gla, state_conv, cache_kv_latent, cache_k_rope, page_table, g_mix_norm, g_ffn_norm, g_final_norm, ab_w_in, gla_w_gate2, gla_b_gate, gla_g_out, conv_w, ab_w_out, mla_w_in, mla_g_q, mla_g_kv, mla_w_uq, mla_w_uk, mla_w_uv, mla_w_out, moe_w_group, moe_b_group, moe_w_expert, moe_b_expert, moe_w_gate, moe_w_up, moe_w_down):
    n_p, len_p, d = x_prompt.shape
    n_s, len_s, _ = x_sample.shape
    t_p, t_s = n_p * len_p, n_s * len_s
    past = page_table.shape[1] * PAGE_SIZE
    x = jnp.concatenate([x_prompt.reshape(t_p, d), x_sample.reshape(t_s, d)], axis=0)
    moe = (moe_w_group, moe_b_group, moe_w_expert, moe_b_expert, moe_w_gate, moe_w_up, moe_w_down)

    h = rmsnorm(x, g_mix_norm[0], BF16)
    g0 = 2 * GLA_KW + GLA_VW
    w_main = jnp.concatenate([ab_w_in[0][:, :g0], ab_w_in[0][:, g0 + GLA_GATE_RANK:]], axis=1)
    w_lr = jnp.pad(ab_w_in[0][:, g0:g0 + GLA_GATE_RANK], ((0, 0), (0, LANES - GLA_GATE_RANK)))
    proj = matmul_ws(h, w_main, (), 0, AB_MAIN_COLS, 512, 512, F32)
    gate_lr = matmul_ws(h, w_lr, (), 0, LANES, 1024, LANES, F32)
    w2_pad = jnp.pad(gla_w_gate2[0], ((0, LANES - GLA_GATE_RANK), (0, 0)))
    b_gate = gla_b_gate[0].reshape(1, GLA_KW)
    g_out = gla_g_out[0].reshape(1, GLA_VW)
    o_p, gla_state_p = gla(proj, gate_lr, w2_pad, b_gate, g_out, None, 0, n_p, len_p, 0, GLA_CHUNK)
    o_s, gla_state_s = gla(proj, gate_lr, w2_pad, b_gate, g_out, state_gla, 0, n_s, len_s, t_p, len_s)
    zero_conv = jnp.zeros((1, n_p, CONV_WIDTH - 1, CONV_CH), F32)
    y_p, conv_state_p = short_conv(proj, zero_conv, conv_w, 0, 0, n_p, len_p, 0, 256)
    y_s, conv_state_s = short_conv(proj, state_conv, conv_w, 0, 0, n_s, len_s, t_p, CONV_CH)
    mixed = jnp.concatenate([jnp.concatenate([o_p, o_s], axis=0),
                             jnp.concatenate([y_p, y_s], axis=0)], axis=1).astype(BF16)
    x = matmul_ws(mixed, ab_w_out, (0,), 0, d, 512, 512, F32, res=x)
    x = moe_layer(x, 0, g_ffn_norm[0], *moe, g_final_norm, False)

    h = rmsnorm(x, g_mix_norm[1], BF16)
    c = matmul_ws(h, mla_w_in, (0,), 0, Q_LORA + KV_LORA, 512, 512, F32)
    w_pe = jnp.pad(mla_w_in[0][:, Q_LORA + KV_LORA:], ((0, 0), (0, LANES - ROPE)))
    kpe_raw = matmul_ws(h, w_pe, (), 0, LANES, 1024, LANES, F32)
    pos = jnp.concatenate([jnp.tile(jnp.arange(len_p), n_p), jnp.tile(past + jnp.arange(len_s), n_s)])
    cos, sin = _rope_tables(pos)
    cqn, lat, lat_b, pe, pe_b = mla_post(c, kpe_raw, cos, sin, mla_g_q[0].reshape(1, -1), mla_g_kv[0].reshape(1, -1))
    w_uq = mla_w_uq[0].reshape(Q_LORA, MLA_HEADS, NOPE + ROPE)
    w_uq_rope = jnp.pad(w_uq[:, :, NOPE:], ((0, 0), (0, 0), (0, LANES - ROPE)))
    w_uq_re = jnp.concatenate([w_uq[:, :, :NOPE].reshape(Q_LORA, -1), w_uq_rope.reshape(Q_LORA, -1)], axis=1)
    q = matmul_ws(cqn, w_uq_re, (), 0, w_uq_re.shape[1], 1024, 1024, F32)
    qs = q_post(q, cos, sin)

    kn = matmul_ws(lat_b[:t_p], mla_w_uk, (0,), 0, MLA_HEADS * NOPE, 1024, 1024, BF16)
    vv = matmul_ws(lat_b[:t_p], mla_w_uv, (0,), 0, MLA_HEADS * V_DIM, 1024, 1024, BF16)
    attn_p = flash_prompt(qs, kn, pe_b, vv, n_p, len_p)

    n_rows = len_s * MLA_HEADS
    q_lat = absorb_q(qs, mla_w_uk, 0, t_p, t_s).reshape(n_s, n_rows, KV_LORA)
    q_pe = qs[t_p:, MLA_HEADS * NOPE:].reshape(n_s, n_rows, LANES)
    o_lat = decode_attention(page_table, q_lat, q_pe, cache_kv_latent, cache_k_rope, lat, pe, 0, t_p, len_s)
    attn_s = expand_v(o_lat.reshape(t_s, MLA_HEADS * KV_LORA), mla_w_uv, 0)
    x = matmul_ws(jnp.concatenate([attn_p, attn_s], axis=0), mla_w_out, (0,), 0, d, 512, 512, F32, res=x)
    y = moe_layer(x, 1, g_ffn_norm[1], *moe, g_final_norm, True)

    return (y[:t_p].reshape(n_p, len_p, d), y[t_p:].reshape(n_s, len_s, d),
            gla_state_p, conv_state_p,
            lat[:t_p].reshape(1, n_p, len_p, KV_LORA), pe[:t_p, :ROPE].reshape(1, n_p, len_p, ROPE),
            gla_state_s, conv_state_s,
            lat[t_p:].reshape(1, n_s, len_s, KV_LORA), pe[t_p:, :ROPE].reshape(1, n_s, len_s, ROPE))
```

```python
import functools

import jax
import jax.numpy as jnp
from jax import lax
from jax.experimental import pallas as pl
from jax.experimental.pallas import tpu as pltpu

F32, BF16, I32 = jnp.float32, jnp.bfloat16, jnp.int32

D_MODEL = 4096
RMS_EPS = 1e-6
PAGE_SIZE = 128

GLA_HEADS = 4
GLA_VW = D_MODEL // 2
GLA_KW = GLA_VW // 2
GLA_DV = GLA_VW // GLA_HEADS
GLA_DK = GLA_KW // GLA_HEADS
GLA_GATE_RANK = 16
GLA_TAU = 16.0
GLA_CHUNK = 64
CONV_CH = D_MODEL // 2
CONV_WIDTH = 3

MLA_HEADS = D_MODEL // 128
Q_LORA = D_MODEL // 4
KV_LORA = 512
NOPE = 128
ROPE = 64
V_DIM = 128
MLA_SCALE = (NOPE + ROPE) ** -0.5
ROPE_THETA = 10000.0

N_GROUPS = 4
EXP_PER_GROUP = 8
N_EXPERTS = N_GROUPS * EXP_PER_GROUP
TOP_K = 2
D_EXPERT = D_MODEL // 4

LANES = 128
SUBLANES = 8
VMEM_LIMIT_BYTES = 56 * 1024 * 1024

COL_Q, COL_K, COL_V, COL_R = 0, GLA_KW, 2 * GLA_KW, 2 * GLA_KW + GLA_VW
COL_B = COL_R + GLA_VW
COL_C = COL_B + CONV_CH
COL_U = COL_C + CONV_CH
AB_MAIN_COLS = COL_U + CONV_CH
AB_GATE_COL0 = 2 * GLA_KW + GLA_VW

MOE_TM = 256
MOE_TF = 512
MOE_TN = 2048
GATHER_TB = 128
GATHER_UNROLL = 8
PAGES_PER_STEP = 16
DECODE_CHUNK = 512
DECODE_UPDATE_CHUNKS = 2


def _params(n_axes):
    return pltpu.CompilerParams(dimension_semantics=("arbitrary",) * n_axes,
                                vmem_limit_bytes=VMEM_LIMIT_BYTES)


def _dot(a, b):
    return jnp.dot(a, b, preferred_element_type=F32)


def _dot_nt(a, b):
    return lax.dot_general(a, b, (((1,), (1,)), ((), ())), preferred_element_type=F32)


def _dot_tn(a, b):
    return lax.dot_general(a, b, (((0,), (0,)), ((), ())), preferred_element_type=F32)


def _rms(x, g):
    return x * lax.rsqrt(jnp.mean(x * x, axis=-1, keepdims=True) + RMS_EPS) * g


def _split3(x):
    hi = x.astype(BF16)
    r1 = x - hi.astype(F32)
    mid = r1.astype(BF16)
    lo = (r1 - mid.astype(F32)).astype(BF16)
    return hi, mid, lo


def _rmsnorm_body(x_ref, g_ref, o_ref):
    o_ref[...] = _rms(x_ref[...], g_ref[...]).astype(o_ref.dtype)


def rmsnorm(x, g, out_dtype, tm=256):
    t, d = x.shape
    return pl.pallas_call(
        _rmsnorm_body, grid=(t // tm,),
        in_specs=[pl.BlockSpec((tm, d), lambda i: (i, 0)), pl.BlockSpec((1, d), lambda i: (0, 0))],
        out_specs=pl.BlockSpec((tm, d), lambda i: (i, 0)),
        out_shape=jax.ShapeDtypeStruct((t, d), out_dtype),
        compiler_params=_params(1), name="rmsnorm")(x, g.reshape(1, d))


def _mm_body(*refs, has_res, w_rows_are_outputs, split_tile):
    a_ref, w_ref = refs[0], refs[1]
    refs = refs[2:]
    a2_ref = None
    if split_tile is not None:
        a2_ref, refs = refs[0], refs[1:]
    if has_res:
        res_ref, o_ref, wb_ref = refs
    else:
        o_ref, wb_ref = refs

    @pl.when(pl.program_id(1) == 0)
    def _cast_weights():
        w = w_ref[...]
        wb_ref[...] = (w.T if w_rows_are_outputs else w).astype(BF16)

    def tile(lhs_ref):
        acc = _dot(lhs_ref[...], wb_ref[...])
        if has_res:
            acc = acc + res_ref[...]
        o_ref[...] = acc.astype(o_ref.dtype)

    if split_tile is None:
        tile(a_ref)
    else:
        pl.when(pl.program_id(1) < split_tile)(lambda: tile(a_ref))
        pl.when(pl.program_id(1) >= split_tile)(lambda: tile(a2_ref))


def matmul_ws(a, w, lead, col0, n_cols, tm, tn, out_dtype, res=None, a2=None):
    m, k = a.shape
    cb0 = col0 // tn
    wspec = pl.BlockSpec((None,) * len(lead) + (k, tn), lambda j, i: tuple(lead) + (0, cb0 + j))
    split_tile = None
    if a2 is None:
        in_specs = [pl.BlockSpec((tm, k), lambda j, i: (i, 0)), wspec]
        args = [a, w]
    else:
        split_tile = m // tm
        m = m + a2.shape[0]
        in_specs = [pl.BlockSpec((tm, k), lambda j, i: (jnp.minimum(i, split_tile - 1), 0)), wspec,
                    pl.BlockSpec((tm, k), lambda j, i: (jnp.maximum(i - split_tile, 0), 0))]
        args = [a, w, a2]
    if res is not None:
        in_specs.append(pl.BlockSpec((tm, tn), lambda j, i: (i, j)))
        args.append(res)
    return pl.pallas_call(
        functools.partial(_mm_body, has_res=res is not None, w_rows_are_outputs=False, split_tile=split_tile),
        grid=(n_cols // tn, m // tm), in_specs=in_specs,
        out_specs=pl.BlockSpec((tm, tn), lambda j, i: (i, j)),
        out_shape=jax.ShapeDtypeStruct((m, n_cols), out_dtype),
        scratch_shapes=[pltpu.VMEM((k, tn), BF16)],
        compiler_params=_params(2), name="matmul_ws")(*args)


def matmul_wt(a, wt, row_of_tile, n_cols, tm, tn, out_dtype):
    m, k = a.shape
    wspec = pl.BlockSpec((pl.Element(tn), pl.Element(k)),
                         lambda j, i: (pl.multiple_of(row_of_tile(j), SUBLANES), 0))
    return pl.pallas_call(
        functools.partial(_mm_body, has_res=False, w_rows_are_outputs=True, split_tile=None),
        grid=(n_cols // tn, m // tm),
        in_specs=[pl.BlockSpec((tm, k), lambda j, i: (i, 0)), wspec],
        out_specs=pl.BlockSpec((tm, tn), lambda j, i: (i, j)),
        out_shape=jax.ShapeDtypeStruct((m, n_cols), out_dtype),
        scratch_shapes=[pltpu.VMEM((k, tn), BF16)],
        compiler_params=_params(2), name="matmul_wt")(a, wt)


def _log_sigmoid(x):
    return jnp.minimum(x, 0.0) - jnp.log1p(jnp.exp(-jnp.abs(x)))


def _gla_body(*refs, chunk, has_s0):
    q_ref, k_ref, v_ref, r_ref, g_ref, w2_ref, bg_ref, go_ref = refs[:8]
    if has_s0:
        s0_ref, o_ref, s_ref = refs[8:]
    else:
        o_ref, s_ref = refs[8:]

    @pl.when(pl.program_id(1) == 0)
    def _init_state():
        s_ref[...] = s0_ref[...] if has_s0 else jnp.zeros_like(s_ref)

    gate_in = _dot(g_ref[...].astype(BF16), w2_ref[...].astype(BF16)) + bg_ref[...]
    log_a = _log_sigmoid(gate_in) / GLA_TAU

    row = lax.broadcasted_iota(I32, (chunk, chunk), 0)
    col = lax.broadcasted_iota(I32, (chunk, chunk), 1)
    causal = row >= col
    tri = jnp.where(causal, 1.0, 0.0).astype(BF16)
    ones = jnp.ones((chunk, GLA_DV), BF16)
    parts = _split3(log_a)
    b_all = _dot(tri, parts[0]) + _dot(tri, parts[1]) + _dot(tri, parts[2])

    for h in range(GLA_HEADS):
        ks = slice(h * GLA_DK, (h + 1) * GLA_DK)
        vs = slice(h * GLA_DV, (h + 1) * GLA_DV)
        b = b_all[:, ks]
        b_last = b[chunk - 1:chunk, :]
        b_tot_col = (_dot_tn(parts[0][:, ks], ones) + _dot_tn(parts[1][:, ks], ones)
                     + _dot_tn(parts[2][:, ks], ones))
        q = q_ref[:, ks] * (GLA_DK ** -0.5)
        k = k_ref[:, ks]
        v = v_ref[:, vs].astype(BF16)
        q_dec = (q * jnp.exp(b)).astype(BF16)
        k_inv = (k * jnp.exp(-b)).astype(BF16)
        k_end = (k * jnp.exp(b_last - b)).astype(BF16)
        att = jnp.where(causal, _dot_nt(q_dec, k_inv), 0.0).astype(BF16)
        s_old = s_ref[h]
        o = _dot(q_dec, s_old.astype(BF16)) + _dot(att, v)
        s_ref[h] = jnp.exp(b_tot_col) * s_old + _dot_tn(k_end, v)
        r = r_ref[:, vs]
        o_ref[:, vs] = (_rms(o, go_ref[:, vs]) * (r * jax.nn.sigmoid(r))).astype(o_ref.dtype)


def gla(proj, gate_lr, w2_pad, b_gate, g_out, s0, li, n_seq, seq_len, row0, chunk):
    nc = seq_len // chunk
    rb0 = row0 // chunk

    def rows(b, c):
        return rb0 + b * nc + c

    state_block = (None, None, GLA_HEADS, GLA_DK, GLA_DV)
    in_specs = [
        pl.BlockSpec((chunk, GLA_KW), lambda b, c: (rows(b, c), COL_Q // GLA_KW)),
        pl.BlockSpec((chunk, GLA_KW), lambda b, c: (rows(b, c), COL_K // GLA_KW)),
        pl.BlockSpec((chunk, GLA_VW), lambda b, c: (rows(b, c), COL_V // GLA_VW)),
        pl.BlockSpec((chunk, GLA_VW), lambda b, c: (rows(b, c), COL_R // GLA_VW)),
        pl.BlockSpec((chunk, LANES), lambda b, c: (rows(b, c), 0)),
        pl.BlockSpec((LANES, GLA_KW), lambda b, c: (0, 0)),
        pl.BlockSpec((1, GLA_KW), lambda b, c: (0, 0)),
        pl.BlockSpec((1, GLA_VW), lambda b, c: (0, 0)),
    ]
    args = [proj, proj, proj, proj, gate_lr, w2_pad, b_gate, g_out]
    if s0 is not None:
        in_specs.append(pl.BlockSpec(state_block, lambda b, c: (li, b, 0, 0, 0)))
        args.append(s0)
    return pl.pallas_call(
        functools.partial(_gla_body, chunk=chunk, has_s0=s0 is not None),
        grid=(n_seq, nc), in_specs=in_specs,
        out_specs=[pl.BlockSpec((chunk, GLA_VW), lambda b, c: (b * nc + c, 0)),
                   pl.BlockSpec(state_block, lambda b, c: (0, b, 0, 0, 0))],
        out_shape=[jax.ShapeDtypeStruct((n_seq * seq_len, GLA_VW), F32),
                   jax.ShapeDtypeStruct((1, n_seq, GLA_HEADS, GLA_DK, GLA_DV), F32)],
        compiler_params=_params(2), name="gla")(*args)


def _conv_body(gb_ref, gc_ref, u_ref, buf_ref, w_ref, o_ref, st_ref, sc_ref, *, seq_len):
    cu = gc_ref[...] * u_ref[...]
    sc_ref[0:8, :] = jnp.zeros((8, cu.shape[1]), F32)
    sc_ref[6:8, :] = buf_ref[...]
    sc_ref[8:8 + seq_len, :] = cu
    w = w_ref[...]
    z = sc_ref[6:6 + seq_len, :] * w[0:1, :] + sc_ref[7:7 + seq_len, :] * w[1:2, :] + cu * w[2:3, :]
    o_ref[...] = gb_ref[...] * z
    st_ref[...] = sc_ref[6 + seq_len:8 + seq_len, :]


def short_conv(proj, buf, conv_w, li_buf, li_w, n_seq, seq_len, row0, ct):
    rb0 = row0 // seq_len
    return pl.pallas_call(
        functools.partial(_conv_body, seq_len=seq_len),
        grid=(n_seq, CONV_CH // ct),
        in_specs=[
            pl.BlockSpec((seq_len, ct), lambda b, j: (rb0 + b, COL_B // ct + j)),
            pl.BlockSpec((seq_len, ct), lambda b, j: (rb0 + b, COL_C // ct + j)),
            pl.BlockSpec((seq_len, ct), lambda b, j: (rb0 + b, COL_U // ct + j)),
            pl.BlockSpec((None, None, CONV_WIDTH - 1, ct), lambda b, j: (li_buf, b, 0, j)),
            pl.BlockSpec((None, CONV_WIDTH, ct), lambda b, j: (li_w, 0, j)),
        ],
        out_specs=[pl.BlockSpec((seq_len, ct), lambda b, j: (b, j)),
                   pl.BlockSpec((None, None, CONV_WIDTH - 1, ct), lambda b, j: (0, b, 0, j))],
        out_shape=[jax.ShapeDtypeStruct((n_seq * seq_len, CONV_CH), F32),
                   jax.ShapeDtypeStruct((1, n_seq, CONV_WIDTH - 1, CONV_CH), F32)],
        scratch_shapes=[pltpu.VMEM((seq_len + 8, ct), F32)],
        compiler_params=_params(2), name="short_conv")(proj, proj, proj, buf, conv_w)


def _swap_halves(x, lane):
    return jnp.where(lane % ROPE < ROPE // 2, pltpu.roll(x, LANES - ROPE // 2, 1), pltpu.roll(x, ROPE // 2, 1))


def _mla_post_body(cq_ref, ckv_ref, kpe_ref, cos_ref, sin_ref, gq_ref, gkv_ref,
                   cqn_ref, lat_ref, latb_ref, pe_ref, peb_ref):
    cqn_ref[...] = _rms(cq_ref[...], gq_ref[...]).astype(BF16)
    lat = _rms(ckv_ref[...], gkv_ref[...])
    lat_ref[...] = lat
    latb_ref[...] = lat.astype(BF16)
    kpe = kpe_ref[...]
    lane = lax.broadcasted_iota(I32, kpe.shape, 1)
    pe = kpe * cos_ref[...] + _swap_halves(kpe, lane) * sin_ref[...]
    pe_ref[...] = pe
    peb_ref[...] = pe.astype(BF16)


def mla_post(c, kpe_raw, cos, sin, g_q, g_kv, tm=256):
    t = c.shape[0]
    return pl.pallas_call(
        _mla_post_body, grid=(t // tm,),
        in_specs=[
            pl.BlockSpec((tm, Q_LORA), lambda i: (i, 0)),
            pl.BlockSpec((tm, KV_LORA), lambda i: (i, Q_LORA // KV_LORA)),
            pl.BlockSpec((tm, LANES), lambda i: (i, 0)),
            pl.BlockSpec((tm, LANES), lambda i: (i, 0)),
            pl.BlockSpec((tm, LANES), lambda i: (i, 0)),
            pl.BlockSpec((1, Q_LORA), lambda i: (0, 0)),
            pl.BlockSpec((1, KV_LORA), lambda i: (0, 0)),
        ],
        out_specs=[
            pl.BlockSpec((tm, Q_LORA), lambda i: (i, 0)),
            pl.BlockSpec((tm, KV_LORA), lambda i: (i, 0)),
            pl.BlockSpec((tm, KV_LORA), lambda i: (i, 0)),
            pl.BlockSpec((tm, LANES), lambda i: (i, 0)),
            pl.BlockSpec((tm, LANES), lambda i: (i, 0)),
        ],
        out_shape=[
            jax.ShapeDtypeStruct((t, Q_LORA), BF16),
            jax.ShapeDtypeStruct((t, KV_LORA), F32),
            jax.ShapeDtypeStruct((t, KV_LORA), BF16),
            jax.ShapeDtypeStruct((t, LANES), F32),
            jax.ShapeDtypeStruct((t, LANES), BF16),
        ],
        compiler_params=_params(1), name="mla_post")(c, c, kpe_raw, cos, sin, g_q, g_kv)


def _q_proj_body(a_ref, w_ref, cos_ref, sin_ref, o_ref, wb_ref, *, n_nope_tiles):
    j = pl.program_id(0)

    @pl.when(pl.program_id(1) == 0)
    def _cast_weights():
        wb_ref[...] = w_ref[...].astype(BF16)

    acc = _dot(a_ref[...], wb_ref[...])

    @pl.when(j < n_nope_tiles)
    def _nope():
        o_ref[...] = (acc * MLA_SCALE).astype(BF16)

    @pl.when(j >= n_nope_tiles)
    def _rope():
        cos, sin = cos_ref[...], sin_ref[...]
        lane = lax.broadcasted_iota(I32, cos.shape, 1)
        for c in range(acc.shape[1] // LANES):
            x = acc[:, c * LANES:(c + 1) * LANES]
            y = x * cos + _swap_halves(x, lane) * sin
            o_ref[:, c * LANES:(c + 1) * LANES] = (y * MLA_SCALE).astype(BF16)


def q_proj(cqn, w_uq_re, cos, sin, tm=1024, tn=1024):
    m, k = cqn.shape
    n = w_uq_re.shape[1]
    return pl.pallas_call(
        functools.partial(_q_proj_body, n_nope_tiles=MLA_HEADS * NOPE // tn), grid=(n // tn, m // tm),
        in_specs=[pl.BlockSpec((tm, k), lambda j, i: (i, 0)),
                  pl.BlockSpec((k, tn), lambda j, i: (0, j)),
                  pl.BlockSpec((tm, LANES), lambda j, i: (i, 0)),
                  pl.BlockSpec((tm, LANES), lambda j, i: (i, 0))],
        out_specs=pl.BlockSpec((tm, tn), lambda j, i: (i, j)),
        out_shape=jax.ShapeDtypeStruct((m, n), BF16),
        scratch_shapes=[pltpu.VMEM((k, tn), BF16)],
        compiler_params=_params(2), name="q_proj")(cqn, w_uq_re, cos, sin)


def _flash_body(qn_ref, qp_ref, kn_ref, kp_ref, v_ref, o_ref, *, tq):
    qi = pl.program_id(2)
    heads = [slice(h * NOPE, (h + 1) * NOPE) for h in range(2)]
    qn = [qn_ref[:, hs] for hs in heads]
    qp = [qp_ref[:, hs] for hs in heads]
    on_or_below_diag = (lax.broadcasted_iota(I32, (tq, tq), 0) >= lax.broadcasted_iota(I32, (tq, tq), 1))

    def block(j, carry, diagonal):
        ks = pl.ds(pl.multiple_of(j * tq, tq), tq)
        kp = kp_ref[ks, :]
        out = []
        for h, hs in enumerate(heads):
            m, l, acc = carry[h]
            s = _dot_nt(qn[h], kn_ref[ks, hs]) + _dot_nt(qp[h], kp)
            if diagonal:
                s = jnp.where(on_or_below_diag, s, -jnp.inf)
            m_new = jnp.maximum(m, jnp.max(s, axis=1, keepdims=True))
            alpha = jnp.exp(m - m_new)
            p = jnp.exp(s - m_new)
            l = alpha * l + jnp.sum(p, axis=1, keepdims=True)
            acc = alpha * acc + _dot(p.astype(BF16), v_ref[ks, hs])
            out.append((m_new, l, acc))
        return tuple(out)

    init = tuple((jnp.full((tq, 1), -jnp.inf, F32), jnp.zeros((tq, 1), F32), jnp.zeros((tq, V_DIM), F32))
                 for _ in heads)
    carry = lax.fori_loop(0, qi, lambda j, c: block(j, c, False), init)
    carry = block(qi, carry, True)
    for h, hs in enumerate(heads):
        _, l, acc = carry[h]
        o_ref[:, hs] = (acc / l).astype(o_ref.dtype)


def flash_prompt(qs, kn, kp, v, n_seq, seq_len, tq=512):
    nq = seq_len // tq
    hw = 2 * NOPE
    rope_cb0 = MLA_HEADS * NOPE // hw
    return pl.pallas_call(
        functools.partial(_flash_body, tq=tq),
        grid=(n_seq, MLA_HEADS // 2, nq),
        in_specs=[
            pl.BlockSpec((tq, hw), lambda b, h, i: (b * nq + i, h)),
            pl.BlockSpec((tq, hw), lambda b, h, i: (b * nq + i, rope_cb0 + h)),
            pl.BlockSpec((seq_len, hw), lambda b, h, i: (b, h)),
            pl.BlockSpec((seq_len, LANES), lambda b, h, i: (b, 0)),
            pl.BlockSpec((seq_len, hw), lambda b, h, i: (b, h)),
        ],
        out_specs=pl.BlockSpec((tq, hw), lambda b, h, i: (b * nq + i, h)),
        out_shape=jax.ShapeDtypeStruct((n_seq * seq_len, MLA_HEADS * V_DIM), BF16),
        compiler_params=_params(3), name="flash_prompt")(qs, qs, kn, kp, v)


def _absorb_q_body(qn_ref, qp_ref, w_ref, ql_ref, qpo_ref, *, n_seq, dec_seq):
    w = w_ref[...].astype(BF16)
    ql_ref[...] = _dot_nt(qn_ref[...], w).reshape(n_seq, dec_seq, KV_LORA)
    qpo_ref[...] = qp_ref[...].astype(F32).reshape(n_seq, dec_seq, LANES)


def absorb_q(qs, w_uk, li, row0, n_seq, dec_seq):
    n_rows = n_seq * dec_seq
    rope_cb0 = MLA_HEADS * NOPE // LANES
    return pl.pallas_call(
        functools.partial(_absorb_q_body, n_seq=n_seq, dec_seq=dec_seq), grid=(MLA_HEADS,),
        in_specs=[pl.BlockSpec((n_rows, NOPE), lambda h: (row0 // n_rows, h)),
                  pl.BlockSpec((n_rows, LANES), lambda h: (row0 // n_rows, rope_cb0 + h)),
                  pl.BlockSpec((None, KV_LORA, NOPE), lambda h: (li, 0, h))],
        out_specs=[pl.BlockSpec((n_seq, dec_seq, KV_LORA), lambda h: (0, h, 0)),
                   pl.BlockSpec((n_seq, dec_seq, LANES), lambda h: (0, h, 0))],
        out_shape=[jax.ShapeDtypeStruct((n_seq, MLA_HEADS * dec_seq, KV_LORA), F32),
                   jax.ShapeDtypeStruct((n_seq, MLA_HEADS * dec_seq, LANES), F32)],
        compiler_params=_params(1), name="absorb_q")(qs, qs, w_uk)


def _expand_v_body(a_ref, w_ref, o_ref):
    n_seq, dec_seq, _ = a_ref.shape
    a = a_ref[...].reshape(n_seq * dec_seq, KV_LORA).astype(BF16)
    o_ref[...] = _dot(a, w_ref[...].astype(BF16)).astype(o_ref.dtype)


def expand_v(o_lat, w_uv, li, dec_seq):
    n_seq = o_lat.shape[0]
    n_rows = n_seq * dec_seq
    return pl.pallas_call(
        _expand_v_body, grid=(MLA_HEADS,),
        in_specs=[pl.BlockSpec((n_seq, dec_seq, KV_LORA), lambda h: (0, h, 0)),
                  pl.BlockSpec((None, KV_LORA, V_DIM), lambda h: (li, 0, h))],
        out_specs=pl.BlockSpec((n_rows, V_DIM), lambda h: (0, h)),
        out_shape=jax.ShapeDtypeStruct((n_rows, MLA_HEADS * V_DIM), BF16),
        compiler_params=_params(1), name="expand_v")(o_lat, w_uv)


def _decode_body(pt_ref, ql_ref, qp_ref, latn_ref, pen_ref, lat_hbm, pe_hbm, o_ref,
                 lat_buf, pe_buf, sem, qlb_ref, qpb_ref, m_ref, l_ref, acc_ref, *, li, n_pages, dec_seq):
    s_idx = pl.program_id(1)
    n_steps = pl.num_programs(1)
    step = pl.program_id(0) * n_steps + s_idx
    n_rows = ql_ref.shape[0]

    def page_copies(seq, kv_step, slot):
        out = []
        for kk in range(n_pages):
            page = pt_ref[seq, kv_step * n_pages + kk]
            keys = pl.ds(kk * PAGE_SIZE, PAGE_SIZE)
            out.append(pltpu.make_async_copy(lat_hbm.at[li, page], lat_buf.at[slot, keys], sem.at[0, slot]))
            out.append(pltpu.make_async_copy(pe_hbm.at[li, page], pe_buf.at[slot, :, keys], sem.at[1, slot]))
        return out

    @pl.when(step == 0)
    def _first():
        for cp in page_copies(0, 0, 0):
            cp.start()

    @pl.when(step + 1 < pl.num_programs(0) * n_steps)
    def _prefetch():
        nxt = step + 1
        for cp in page_copies(nxt // n_steps, nxt % n_steps, nxt % 2):
            cp.start()

    @pl.when(s_idx == 0)
    def _init():
        qlb_ref[...] = ql_ref[...].astype(BF16)
        qpb_ref[...] = qp_ref[...].astype(BF16)
        m_ref[...] = jnp.full(m_ref.shape, -jnp.inf, F32)
        l_ref[...] = jnp.zeros(l_ref.shape, F32)
        acc_ref[...] = jnp.zeros(acc_ref.shape, F32)

    ql = qlb_ref[...]
    qp = qpb_ref[:, :ROPE]

    def update(scores, keys):
        m_prev = m_ref[...]
        m_new = m_prev
        for s in scores:
            m_new = jnp.maximum(m_new, jnp.max(s, axis=1, keepdims=True))
        alpha = jnp.exp(m_prev - m_new)
        l_new = alpha * l_ref[...]
        acc = alpha * acc_ref[...]
        for s, kc in zip(scores, keys):
            p = jnp.exp(s - m_new)
            l_new = l_new + jnp.sum(p, axis=1, keepdims=True)
            acc = acc + _dot(p.astype(BF16), kc)
        m_ref[...] = m_new
        l_ref[...] = l_new
        acc_ref[...] = acc

    slot = step % 2
    for cp in page_copies(pl.program_id(0), s_idx, slot):
        cp.wait()
    lat_slot = lat_buf.at[slot]
    pe_slot = pe_buf.at[slot]
    scores, keys = [], []
    for c in range(n_pages * PAGE_SIZE // DECODE_CHUNK):
        cs = slice(c * DECODE_CHUNK, (c + 1) * DECODE_CHUNK)
        kc = lat_slot[cs, :].astype(BF16)
        keys.append(kc)
        scores.append(_dot_nt(ql, kc) + _dot(qp, pe_slot[:, cs].astype(BF16)))
    for c0 in range(0, len(scores), DECODE_UPDATE_CHUNKS):
        update(scores[c0:c0 + DECODE_UPDATE_CHUNKS], keys[c0:c0 + DECODE_UPDATE_CHUNKS])

    @pl.when(s_idx == n_steps - 1)
    def _new_tokens():
        pad = PAGE_SIZE - dec_seq
        lat_new = jnp.concatenate([latn_ref[...], jnp.zeros((pad, KV_LORA), F32)], axis=0).astype(BF16)
        pe_new = jnp.concatenate([pen_ref[:, :ROPE], jnp.zeros((pad, ROPE), F32)], axis=0).astype(BF16)
        s = _dot_nt(ql, lat_new) + _dot_nt(qp, pe_new)
        q_tok = lax.broadcasted_iota(I32, (n_rows, PAGE_SIZE), 0) % dec_seq
        key_tok = lax.broadcasted_iota(I32, (n_rows, PAGE_SIZE), 1)
        update([jnp.where(key_tok <= q_tok, s, -jnp.inf)], [lat_new])
        o_ref[...] = acc_ref[...] / l_ref[...]


def decode_attention(page_table, q_lat, q_pe, cache_lat, cache_pe_t, lat_new, pe_new, li, row0, dec_seq):
    n_seq, n_rows, _ = q_lat.shape
    n_steps = page_table.shape[1] // PAGES_PER_STEP
    n_keys = PAGES_PER_STEP * PAGE_SIZE
    grid_spec = pltpu.PrefetchScalarGridSpec(
        num_scalar_prefetch=1, grid=(n_seq, n_steps),
        in_specs=[pl.BlockSpec((None, n_rows, KV_LORA), lambda b, s, pt: (b, 0, 0)),
                  pl.BlockSpec((None, n_rows, LANES), lambda b, s, pt: (b, 0, 0)),
                  pl.BlockSpec((dec_seq, KV_LORA), lambda b, s, pt: (row0 // dec_seq + b, 0)),
                  pl.BlockSpec((dec_seq, LANES), lambda b, s, pt: (row0 // dec_seq + b, 0)),
                  pl.BlockSpec(memory_space=pl.ANY), pl.BlockSpec(memory_space=pl.ANY)],
        out_specs=pl.BlockSpec((None, n_rows, KV_LORA), lambda b, s, pt: (b, 0, 0)),
        scratch_shapes=[pltpu.VMEM((2, n_keys, KV_LORA), F32), pltpu.VMEM((2, ROPE, n_keys), F32),
                        pltpu.SemaphoreType.DMA((2, 2)),
                        pltpu.VMEM((n_rows, KV_LORA), BF16), pltpu.VMEM((n_rows, LANES), BF16),
                        pltpu.VMEM((n_rows, 1), F32), pltpu.VMEM((n_rows, 1), F32),
                        pltpu.VMEM((n_rows, KV_LORA), F32)])
    return pl.pallas_call(
        functools.partial(_decode_body, li=li, n_pages=PAGES_PER_STEP, dec_seq=dec_seq),
        grid_spec=grid_spec, out_shape=jax.ShapeDtypeStruct((n_seq, n_rows, KV_LORA), F32),
        compiler_params=_params(2), name="decode_attention")(
            page_table, q_lat, q_pe, lat_new, pe_new, cache_lat, cache_pe_t)


def _router_body(x_ref, g_ref, w_ref, b_ref, o_ref):
    h = _rms(x_ref[...], g_ref[...])
    logits = jnp.dot(h, w_ref[...], preferred_element_type=F32, precision=lax.Precision.HIGHEST) + b_ref[...]
    lane = lax.broadcasted_iota(I32, logits.shape, 1)
    lane_f = lane.astype(F32)
    neg = -jnp.inf
    lg = jnp.where(lane < N_GROUPS, logits, neg)
    mg = jnp.max(lg, axis=1, keepdims=True)
    grp = jnp.min(jnp.where(lg == mg, lane_f, float(LANES)), axis=1, keepdims=True)
    p_grp = 1.0 / jnp.sum(jnp.exp(lg - mg), axis=1, keepdims=True)
    e_lane = lane_f - float(N_GROUPS)
    in_grp = (lane >= N_GROUPS) & (lane < N_GROUPS + N_EXPERTS) & (jnp.floor(e_lane / EXP_PER_GROUP) == grp)
    le = jnp.where(in_grp, logits, neg)
    m1 = jnp.max(le, axis=1, keepdims=True)
    i1 = jnp.min(jnp.where(le == m1, lane_f, float(LANES)), axis=1, keepdims=True)
    le2 = jnp.where(lane_f == i1, neg, le)
    m2 = jnp.max(le2, axis=1, keepdims=True)
    i2 = jnp.min(jnp.where(le2 == m2, lane_f, float(LANES)), axis=1, keepdims=True)
    e2 = jnp.exp(m2 - m1)
    g1 = p_grp / (1.0 + e2)
    g2 = p_grp * e2 / (1.0 + e2)
    out = jnp.where(lane == 0, i1 - N_GROUPS, jnp.where(lane == 1, i2 - N_GROUPS,
                    jnp.where(lane == 2, g1, jnp.where(lane == 3, g2, 0.0))))
    o_ref[...] = out


def moe_router(x, g, w_router, b_router, tm=256):
    t, d = x.shape
    return pl.pallas_call(
        _router_body, grid=(t // tm,),
        in_specs=[pl.BlockSpec((tm, d), lambda i: (i, 0)), pl.BlockSpec((1, d), lambda i: (0, 0)),
                  pl.BlockSpec((d, LANES), lambda i: (0, 0)), pl.BlockSpec((1, LANES), lambda i: (0, 0))],
        out_specs=pl.BlockSpec((tm, LANES), lambda i: (i, 0)),
        out_shape=jax.ShapeDtypeStruct((t, LANES), F32),
        compiler_params=_params(1), name="moe_router")(x, g.reshape(1, d), w_router, b_router)


def _row_copy(src_hbm, dst, sem, src_row, dst_row):
    return pltpu.make_async_copy(src_hbm.at[pl.ds(src_row, 1)], dst.at[pl.ds(dst_row, 1)], sem)


def _rows_done(src_hbm, dst, sem):
    return pltpu.make_async_copy(src_hbm.at[pl.ds(0, dst.shape[0])], dst, sem)


def _dispatch_body(tok_ref, nsteps_ref, x_hbm, g_ref, o_ref, buf, sem, *, tb):
    i = pl.program_id(0)
    n_used = nsteps_ref[0]

    def start_all(step, slot):
        def body(r, c):
            _row_copy(x_hbm, buf.at[slot], sem.at[slot], tok_ref[step * tb + r], r).start()
            return c
        lax.fori_loop(0, tb, body, 0, unroll=GATHER_UNROLL)

    @pl.when(i == 0)
    def _first():
        start_all(0, 0)

    @pl.when(i + 1 < n_used)
    def _prefetch():
        start_all(i + 1, (i + 1) % 2)

    @pl.when(i < n_used)
    def _compute():
        slot = i % 2
        _rows_done(x_hbm, buf.at[slot], sem.at[slot]).wait()
        o_ref[...] = _rms(buf[slot], g_ref[...]).astype(o_ref.dtype)

    @pl.when(i >= n_used)
    def _unused_block():
        o_ref[...] = jnp.zeros_like(o_ref)


def moe_dispatch(tok_sorted, n_used_steps, x, g, tb=GATHER_TB):
    n_rows = tok_sorted.shape[0]
    d = x.shape[1]
    grid_spec = pltpu.PrefetchScalarGridSpec(
        num_scalar_prefetch=2, grid=(n_rows // tb,),
        in_specs=[pl.BlockSpec(memory_space=pl.ANY), pl.BlockSpec((1, d), lambda i, tok, ns: (0, 0))],
        out_specs=pl.BlockSpec((tb, d), lambda i, tok, ns: (i, 0)),
        scratch_shapes=[pltpu.VMEM((2, tb, d), F32), pltpu.SemaphoreType.DMA((2,))])
    return pl.pallas_call(
        functools.partial(_dispatch_body, tb=tb), grid_spec=grid_spec,
        out_shape=jax.ShapeDtypeStruct((n_rows, d), BF16),
        compiler_params=_params(1), name="moe_dispatch")(tok_sorted, n_used_steps, x, g.reshape(1, d))


def _weight_tile_stream(tabs, t, hbm_tiles, bufs, sem):
    e_ref, cw_ref, slot_ref, ne_ref, ncw_ref, has_next_ref = tabs

    def copies(e, cw, slot):
        return [pltpu.make_async_copy(src, buf.at[slot], sem.at[n, slot])
                for n, (src, buf) in enumerate(zip(hbm_tiles(e, cw), bufs))]

    slot = slot_ref[t]

    @pl.when(t == 0)
    def _first_group():
        for cp in copies(e_ref[0], cw_ref[0], 0):
            cp.start()

    @pl.when(has_next_ref[t] == 1)
    def _next_group():
        for cp in copies(ne_ref[t], ncw_ref[t], 1 - slot):
            cp.start()

    for cp in copies(e_ref[t], cw_ref[t], slot):
        cp.wait()
    return slot


def _moe_up_body(e_ref, cw_ref, rb_ref, co_ref, first_ref, valid_ref, slot_ref, ne_ref, ncw_ref, hn_ref,
                 x_ref, wg_hbm, wu_hbm, o_ref, wg_buf, wu_buf, sem, wgb, wub, *, layer):
    t = pl.program_id(0)

    @pl.when(first_ref[t] == 1)
    def _new_weights():
        def tiles(e, cw):
            cols = pl.ds(pl.multiple_of(cw * MOE_TF, MOE_TF), MOE_TF)
            return [wg_hbm.at[layer, e, :, cols], wu_hbm.at[layer, e, :, cols]]
        slot = _weight_tile_stream((e_ref, cw_ref, slot_ref, ne_ref, ncw_ref, hn_ref), t, tiles,
                                   (wg_buf, wu_buf), sem)
        wgb[...] = wg_buf[slot].astype(BF16)
        wub[...] = wu_buf[slot].astype(BF16)

    @pl.when(valid_ref[t] == 1)
    def _compute():
        x = x_ref[...]
        a = _dot(x, wgb[...])
        u = _dot(x, wub[...])
        o_ref[...] = (a * jax.nn.sigmoid(a) * u).astype(o_ref.dtype)

    @pl.when(valid_ref[t] == 0)
    def _unused_block():
        o_ref[...] = jnp.zeros_like(o_ref)


def moe_up(items, xs, w_gate, w_up, layer):
    n_rows, d = xs.shape
    n_items = items[0].shape[0]
    n_tabs = len(items)
    grid_spec = pltpu.PrefetchScalarGridSpec(
        num_scalar_prefetch=n_tabs, grid=(n_items,),
        in_specs=[pl.BlockSpec((MOE_TM, d), lambda t, *tabs: (tabs[2][t], 0)),
                  pl.BlockSpec(memory_space=pl.ANY), pl.BlockSpec(memory_space=pl.ANY)],
        out_specs=pl.BlockSpec((MOE_TM, MOE_TF), lambda t, *tabs: (tabs[2][t], tabs[3][t])),
        scratch_shapes=[pltpu.VMEM((2, d, MOE_TF), F32), pltpu.VMEM((2, d, MOE_TF), F32),
                        pltpu.SemaphoreType.DMA((2, 2)),
                        pltpu.VMEM((d, MOE_TF), BF16), pltpu.VMEM((d, MOE_TF), BF16)])
    return pl.pallas_call(
        functools.partial(_moe_up_body, layer=layer), grid_spec=grid_spec,
        out_shape=jax.ShapeDtypeStruct((n_rows, D_EXPERT), BF16),
        compiler_params=_params(1), name="moe_up")(*items, xs, w_gate, w_up)


def _moe_down_body(e_ref, cw_ref, rb_ref, co_ref, first_ref, valid_ref, slot_ref, ne_ref, ncw_ref, hn_ref,
                   h_ref, wd_hbm, o_ref, wd_buf, sem, wdb, *, layer):
    t = pl.program_id(0)

    @pl.when(first_ref[t] == 1)
    def _new_weights():
        def tiles(e, cw):
            return [wd_hbm.at[layer, e, :, pl.ds(pl.multiple_of(cw * MOE_TN, MOE_TN), MOE_TN)]]
        slot = _weight_tile_stream((e_ref, cw_ref, slot_ref, ne_ref, ncw_ref, hn_ref), t, tiles, (wd_buf,), sem)
        wdb[...] = wd_buf[slot].astype(BF16)

    @pl.when(valid_ref[t] == 1)
    def _compute():
        o_ref[...] = _dot(h_ref[...], wdb[...]).astype(o_ref.dtype)

    @pl.when(valid_ref[t] == 0)
    def _unused_block():
        o_ref[...] = jnp.zeros_like(o_ref)


def moe_down(items, hmid, w_down, layer):
    n_rows, f = hmid.shape
    d = w_down.shape[-1]
    n_items = items[0].shape[0]
    grid_spec = pltpu.PrefetchScalarGridSpec(
        num_scalar_prefetch=len(items), grid=(n_items,),
        in_specs=[pl.BlockSpec((MOE_TM, f), lambda t, *tabs: (tabs[2][t], 0)),
                  pl.BlockSpec(memory_space=pl.ANY)],
        out_specs=pl.BlockSpec((MOE_TM, MOE_TN), lambda t, *tabs: (tabs[2][t], tabs[3][t])),
        scratch_shapes=[pltpu.VMEM((2, f, MOE_TN), F32), pltpu.SemaphoreType.DMA((1, 2)),
                        pltpu.VMEM((f, MOE_TN), BF16)])
    return pl.pallas_call(
        functools.partial(_moe_down_body, layer=layer), grid_spec=grid_spec,
        out_shape=jax.ShapeDtypeStruct((n_rows, d), F32),
        compiler_params=_params(1), name="moe_down")(*items, hmid, w_down)


def _combine_body(pos_ref, x_ref, gate_ref, gn_ref, yb_hbm, o1_ref, o2_ref, buf, sem, *, tb, final, split_step):
    i = pl.program_id(0)
    n = pl.num_programs(0)

    def start_all(step, slot):
        def body(r, c):
            for kk in range(TOP_K):
                _row_copy(yb_hbm, buf.at[slot, kk], sem.at[slot], pos_ref[(step * tb + r) * TOP_K + kk], r).start()
            return c
        lax.fori_loop(0, tb, body, 0, unroll=GATHER_UNROLL)

    @pl.when(i == 0)
    def _first():
        start_all(0, 0)

    @pl.when(i + 1 < n)
    def _prefetch():
        start_all(i + 1, (i + 1) % 2)

    slot = i % 2
    for kk in range(TOP_K):
        _rows_done(yb_hbm, buf.at[slot, kk], sem.at[slot]).wait()
    gate = gate_ref[...]
    y = x_ref[...] + (gate[:, 0:1] * buf[slot, 0] + gate[:, 1:2] * buf[slot, 1])
    if final:
        yn = _rms(y, gn_ref[...])

        @pl.when(i < split_step)
        def _prompt_rows():
            o1_ref[...] = yn

        @pl.when(i >= split_step)
        def _sample_rows():
            o2_ref[...] = yn
    else:
        o1_ref[...] = y
        o2_ref[...] = _rms(y, gn_ref[...]).astype(o2_ref.dtype)


def moe_combine(pos, x, gates, yb, g_norm, final, n_prompt_rows, tb=GATHER_TB):
    t, d = x.shape
    split_step = n_prompt_rows // tb
    if final:
        out_specs = [pl.BlockSpec((tb, d), lambda i, p: (jnp.minimum(i, split_step - 1), 0)),
                     pl.BlockSpec((tb, d), lambda i, p: (jnp.maximum(i - split_step, 0), 0))]
        out_shape = [jax.ShapeDtypeStruct((n_prompt_rows, d), F32),
                     jax.ShapeDtypeStruct((t - n_prompt_rows, d), F32)]
    else:
        out_specs = [pl.BlockSpec((tb, d), lambda i, p: (i, 0)), pl.BlockSpec((tb, d), lambda i, p: (i, 0))]
        out_shape = [jax.ShapeDtypeStruct((t, d), F32), jax.ShapeDtypeStruct((t, d), BF16)]
    grid_spec = pltpu.PrefetchScalarGridSpec(
        num_scalar_prefetch=1, grid=(t // tb,),
        in_specs=[pl.BlockSpec((tb, d), lambda i, p: (i, 0)), pl.BlockSpec((tb, TOP_K), lambda i, p: (i, 0)),
                  pl.BlockSpec((1, d), lambda i, p: (0, 0)), pl.BlockSpec(memory_space=pl.ANY)],
        out_specs=out_specs,
        scratch_shapes=[pltpu.VMEM((2, TOP_K, tb, d), F32), pltpu.SemaphoreType.DMA((2,))])
    return pl.pallas_call(
        functools.partial(_combine_body, tb=tb, final=final, split_step=split_step), grid_spec=grid_spec,
        out_shape=out_shape, compiler_params=_params(1), name="moe_combine")(
            pos, x, gates, g_norm.reshape(1, d), yb)


def _work_items(n_col_tiles, blocks_per_expert, block_start, n_used, n_blocks):
    n_items = n_col_tiles * n_blocks
    idx = jnp.arange(n_items, dtype=I32)
    n_valid = n_col_tiles * n_used
    valid = idx < n_valid
    t = jnp.minimum(idx, n_valid - 1)
    item_end = n_col_tiles * jnp.cumsum(blocks_per_expert)
    e = jnp.minimum(jnp.sum((t[:, None] >= item_end[None, :]).astype(I32), axis=1), N_EXPERTS - 1)
    nb = jnp.maximum(blocks_per_expert[e], 1)
    local = t - n_col_tiles * block_start[e]
    cw = (local // nb).astype(I32)
    r = local % nb
    spare = idx - n_valid
    rb = jnp.where(valid, block_start[e] + r, n_used + spare // n_col_tiles).astype(I32)
    co = jnp.where(valid, cw, spare % n_col_tiles).astype(I32)
    first = (r == 0) & valid
    slot = ((jnp.cumsum(first.astype(I32)) - 1) % 2).astype(I32)
    nxt = idx + nb
    has_next = first & (nxt < n_valid)
    nxt = jnp.minimum(nxt, n_items - 1)
    return (e, cw, rb, co, first.astype(I32), valid.astype(I32), slot, e[nxt], cw[nxt], has_next.astype(I32))


def moe_layer(x, layer, g_norm, w_group, b_group, w_expert, b_expert, w_gate, w_up, w_down,
              g_next, final, n_prompt_rows):
    t, d = x.shape
    pad = LANES - N_GROUPS - N_EXPERTS
    w_router = jnp.concatenate([w_group[layer], w_expert[layer], jnp.zeros((d, pad), F32)], axis=1)
    b_router = jnp.concatenate([b_group[layer], b_expert[layer], jnp.zeros((pad,), F32)]).reshape(1, LANES)
    routed = moe_router(x, g_norm, w_router, b_router)
    eid = routed[:, 0:TOP_K].astype(I32).reshape(-1)
    gates = routed[:, TOP_K:2 * TOP_K]

    n_assign = t * TOP_K
    onehot = (eid[:, None] == jnp.arange(N_EXPERTS, dtype=I32)[None, :]).astype(I32)
    rank = jnp.sum((jnp.cumsum(onehot, axis=0) - onehot) * onehot, axis=1)
    counts = jnp.sum(onehot, axis=0)
    blocks_per_expert = (counts + MOE_TM - 1) // MOE_TM
    block_start = jnp.cumsum(blocks_per_expert) - blocks_per_expert
    n_used = jnp.sum(blocks_per_expert)
    n_blocks = -(-(n_assign + N_EXPERTS * (MOE_TM - 1)) // MOE_TM)
    pos = (block_start[eid] * MOE_TM + rank).astype(I32)
    tok = jnp.arange(n_assign, dtype=I32) // TOP_K
    tok_sorted = jnp.zeros((n_blocks * MOE_TM,), I32).at[pos].set(tok)

    n_used_steps = (n_used * (MOE_TM // GATHER_TB)).astype(I32).reshape(1)
    xs = moe_dispatch(tok_sorted, n_used_steps, x, g_norm)
    hmid = moe_up(_work_items(D_EXPERT // MOE_TF, blocks_per_expert, block_start, n_used, n_blocks),
                  xs, w_gate, w_up, layer)
    yb = moe_down(_work_items(d // MOE_TN, blocks_per_expert, block_start, n_used, n_blocks),
                  hmid, w_down, layer)
    return moe_combine(pos, x, gates, yb, g_next, final, n_prompt_rows)


def _rope_tables(pos):
    inv = ROPE_THETA ** (-jnp.arange(0, ROPE, 2, dtype=F32) / ROPE)
    ang = pos.astype(F32)[:, None] * inv[None, :]
    cos, sin = jnp.cos(ang), jnp.sin(ang)
    return jnp.concatenate([cos] * 4, axis=1), jnp.concatenate([-sin, sin, -sin, sin], axis=1)


def _pad_rows(w, n_rows):
    return jnp.pad(w, ((0, n_rows - w.shape[0]), (0, 0)))


def kernel(x_prompt, x_sample, state_gla, state_conv, cache_kv_latent, cache_k_rope, page_table, g_mix_norm, g_ffn_norm, g_final_norm, ab_w_in, gla_w_gate2, gla_b_gate, gla_g_out, conv_w, ab_w_out, mla_w_in, mla_g_q, mla_g_kv, mla_w_uq, mla_w_uk, mla_w_uv, mla_w_out, moe_w_group, moe_b_group, moe_w_expert, moe_b_expert, moe_w_gate, moe_w_up, moe_w_down):
    n_p, len_p, d = x_prompt.shape
    n_s, len_s, _ = x_sample.shape
    t_p, t_s = n_p * len_p, n_s * len_s
    past = page_table.shape[1] * PAGE_SIZE
    x = jnp.concatenate([x_prompt.reshape(t_p, d), x_sample.reshape(t_s, d)], axis=0)
    moe = (moe_w_group, moe_b_group, moe_w_expert, moe_b_expert, moe_w_gate, moe_w_up, moe_w_down)

    h = rmsnorm(x, g_mix_norm[0], BF16)
    w_in_t = jnp.swapaxes(ab_w_in[0], 0, 1)
    tn = 512
    proj = matmul_wt(h, w_in_t, lambda j: jnp.where(j * tn < AB_GATE_COL0, j * tn, j * tn + GLA_GATE_RANK),
                     AB_MAIN_COLS, 512, tn, F32)
    w_lr_t = _pad_rows(w_in_t[AB_GATE_COL0:AB_GATE_COL0 + GLA_GATE_RANK], LANES)
    gate_lr = matmul_wt(h, w_lr_t, lambda j: j * LANES, LANES, 1024, LANES, F32)
    w2_pad = _pad_rows(gla_w_gate2[0], LANES)
    b_gate = gla_b_gate[0].reshape(1, GLA_KW)
    g_out = gla_g_out[0].reshape(1, GLA_VW)
    o_p, gla_state_p = gla(proj, gate_lr, w2_pad, b_gate, g_out, None, 0, n_p, len_p, 0, GLA_CHUNK)
    o_s, gla_state_s = gla(proj, gate_lr, w2_pad, b_gate, g_out, state_gla, 0, n_s, len_s, t_p, len_s)
    zero_conv = jnp.zeros((1, n_p, CONV_WIDTH - 1, CONV_CH), F32)
    y_p, conv_state_p = short_conv(proj, zero_conv, conv_w, 0, 0, n_p, len_p, 0, 256)
    y_s, conv_state_s = short_conv(proj, state_conv, conv_w, 0, 0, n_s, len_s, t_p, CONV_CH)
    mixed = jnp.concatenate([jnp.concatenate([o_p, o_s], axis=0),
                             jnp.concatenate([y_p, y_s], axis=0)], axis=1).astype(BF16)
    x = matmul_ws(mixed, ab_w_out, (0,), 0, d, 512, 512, F32, res=x)
    x, h = moe_layer(x, 0, g_ffn_norm[0], *moe, g_mix_norm[1], False, t_p)

    w_mla_t = jnp.swapaxes(mla_w_in[0], 0, 1)
    c = matmul_wt(h, w_mla_t, lambda j: j * tn, Q_LORA + KV_LORA, 512, tn, F32)
    w_pe_t = _pad_rows(w_mla_t[Q_LORA + KV_LORA:], LANES)
    kpe_raw = matmul_wt(h, w_pe_t, lambda j: j * LANES, LANES, 1024, LANES, F32)
    pos = jnp.concatenate([jnp.tile(jnp.arange(len_p), n_p), jnp.tile(past + jnp.arange(len_s), n_s)])
    cos, sin = _rope_tables(pos)
    cqn, lat, lat_b, pe, pe_b = mla_post(c, kpe_raw, cos, sin, mla_g_q[0].reshape(1, -1), mla_g_kv[0].reshape(1, -1))
    w_uq = mla_w_uq[0].reshape(Q_LORA, MLA_HEADS, NOPE + ROPE)
    w_uq_rope = jnp.pad(w_uq[:, :, NOPE:], ((0, 0), (0, 0), (0, LANES - ROPE)))
    w_uq_re = jnp.concatenate([w_uq[:, :, :NOPE].reshape(Q_LORA, -1), w_uq_rope.reshape(Q_LORA, -1)], axis=1)
    qs = q_proj(cqn, w_uq_re, cos, sin)

    kn = matmul_ws(lat_b[:t_p], mla_w_uk, (0,), 0, MLA_HEADS * NOPE, 1024, 1024, BF16)
    vv = matmul_ws(lat_b[:t_p], mla_w_uv, (0,), 0, MLA_HEADS * V_DIM, 1024, 1024, BF16)
    attn_p = flash_prompt(qs, kn, pe_b, vv, n_p, len_p)

    q_lat, q_pe = absorb_q(qs, mla_w_uk, 0, t_p, n_s, len_s)
    cache_pe_t = jnp.swapaxes(cache_k_rope, 2, 3)
    o_lat = decode_attention(page_table, q_lat, q_pe, cache_kv_latent, cache_pe_t, lat, pe, 0, t_p, len_s)
    attn_s = expand_v(o_lat, mla_w_uv, 0, len_s)
    x = matmul_ws(attn_p, mla_w_out, (0,), 0, d, 512, 512, F32, res=x, a2=attn_s)
    y_p, y_s = moe_layer(x, 1, g_ffn_norm[1], *moe, g_final_norm, True, t_p)

    return (y_p.reshape(n_p, len_p, d), y_s.reshape(n_s, len_s, d),
            gla_state_p, conv_state_p,
            lat[:t_p].reshape(1, n_p, len_p, KV_LORA), pe[:t_p, :ROPE].reshape(1, n_p, len_p, ROPE),
            gla_state_s, conv_state_s,
            lat[t_p:].reshape(1, n_s, len_s, KV_LORA), pe[t_p:, :ROPE].reshape(1, n_s, len_s, ROPE))
```

```python
import functools

import jax
import jax.numpy as jnp
from jax import lax
from jax.experimental import pallas as pl
from jax.experimental.pallas import tpu as pltpu

F32, BF16, I32 = jnp.float32, jnp.bfloat16, jnp.int32

D_MODEL = 4096
RMS_EPS = 1e-6
PAGE_SIZE = 128

GLA_HEADS = 4
GLA_VW = D_MODEL // 2
GLA_KW = GLA_VW // 2
GLA_DV = GLA_VW // GLA_HEADS
GLA_DK = GLA_KW // GLA_HEADS
GLA_GATE_RANK = 16
GLA_TAU = 16.0
GLA_CHUNK = 64
CONV_CH = D_MODEL // 2
CONV_WIDTH = 3

MLA_HEADS = D_MODEL // 128
Q_LORA = D_MODEL // 4
KV_LORA = 512
NOPE = 128
ROPE = 64
V_DIM = 128
MLA_SCALE = (NOPE + ROPE) ** -0.5
ROPE_THETA = 10000.0

N_GROUPS = 4
EXP_PER_GROUP = 8
N_EXPERTS = N_GROUPS * EXP_PER_GROUP
TOP_K = 2
D_EXPERT = D_MODEL // 4

LANES = 128
SUBLANES = 8
VMEM_LIMIT_BYTES = 56 * 1024 * 1024

COL_Q, COL_K, COL_V, COL_R = 0, GLA_KW, 2 * GLA_KW, 2 * GLA_KW + GLA_VW
COL_B = COL_R + GLA_VW
COL_C = COL_B + CONV_CH
COL_U = COL_C + CONV_CH
AB_MAIN_COLS = COL_U + CONV_CH
AB_GATE_COL0 = 2 * GLA_KW + GLA_VW

MOE_TM = 256
MOE_TF = 512
MOE_TN = 2048
GATHER_TB = 128
GATHER_UNROLL = 8
PAGES_PER_STEP = 32
DECODE_CHUNK = 512
DECODE_UPDATE_CHUNKS = 2


def _params(n_axes):
    return pltpu.CompilerParams(dimension_semantics=("arbitrary",) * n_axes,
                                vmem_limit_bytes=VMEM_LIMIT_BYTES)


def _dot(a, b):
    return jnp.dot(a, b, preferred_element_type=F32)


def _dot_nt(a, b):
    return lax.dot_general(a, b, (((1,), (1,)), ((), ())), preferred_element_type=F32)


def _dot_tn(a, b):
    return lax.dot_general(a, b, (((0,), (0,)), ((), ())), preferred_element_type=F32)


def _rms(x, g):
    return x * lax.rsqrt(jnp.mean(x * x, axis=-1, keepdims=True) + RMS_EPS) * g


def _split3(x):
    hi = x.astype(BF16)
    r1 = x - hi.astype(F32)
    mid = r1.astype(BF16)
    lo = (r1 - mid.astype(F32)).astype(BF16)
    return hi, mid, lo


def _rmsnorm_body(x1_ref, x2_ref, g_ref, o_ref, *, split_step):
    i = pl.program_id(0)

    @pl.when(i < split_step)
    def _first_source():
        o_ref[...] = _rms(x1_ref[...], g_ref[...]).astype(o_ref.dtype)

    @pl.when(i >= split_step)
    def _second_source():
        o_ref[...] = _rms(x2_ref[...], g_ref[...]).astype(o_ref.dtype)


def rmsnorm_stacked(x1, x2, g, out_dtype, tm=256):
    m1, d = x1.shape
    m2 = x2.shape[0]
    split_step = m1 // tm
    return pl.pallas_call(
        functools.partial(_rmsnorm_body, split_step=split_step), grid=((m1 + m2) // tm,),
        in_specs=[pl.BlockSpec((tm, d), lambda i: (jnp.minimum(i, split_step - 1), 0)),
                  pl.BlockSpec((tm, d), lambda i: (jnp.maximum(i - split_step, 0), 0)),
                  pl.BlockSpec((1, d), lambda i: (0, 0))],
        out_specs=pl.BlockSpec((tm, d), lambda i: (i, 0)),
        out_shape=jax.ShapeDtypeStruct((m1 + m2, d), out_dtype),
        compiler_params=_params(1), name="rmsnorm")(x1, x2, g.reshape(1, d))


def _mm_body(*refs, has_res, w_rows_are_outputs, split_tile):
    a_ref, w_ref = refs[0], refs[1]
    refs = refs[2:]
    a2_ref = None
    if split_tile is not None:
        a2_ref, refs = refs[0], refs[1:]
    if has_res:
        res_ref, o_ref, wb_ref = refs
    else:
        o_ref, wb_ref = refs

    @pl.when(pl.program_id(1) == 0)
    def _cast_weights():
        w = w_ref[...]
        wb_ref[...] = (w.T if w_rows_are_outputs else w).astype(BF16)

    def tile(lhs_ref):
        acc = _dot(lhs_ref[...], wb_ref[...])
        if has_res:
            acc = acc + res_ref[...]
        o_ref[...] = acc.astype(o_ref.dtype)

    if split_tile is None:
        tile(a_ref)
    else:
        pl.when(pl.program_id(1) < split_tile)(lambda: tile(a_ref))
        pl.when(pl.program_id(1) >= split_tile)(lambda: tile(a2_ref))


def matmul_ws(a, w, lead, col0, n_cols, tm, tn, out_dtype, res=None, a2=None):
    m, k = a.shape
    cb0 = col0 // tn
    wspec = pl.BlockSpec((None,) * len(lead) + (k, tn), lambda j, i: tuple(lead) + (0, cb0 + j))
    split_tile = None
    if a2 is None:
        in_specs = [pl.BlockSpec((tm, k), lambda j, i: (i, 0)), wspec]
        args = [a, w]
    else:
        split_tile = m // tm
        m = m + a2.shape[0]
        in_specs = [pl.BlockSpec((tm, k), lambda j, i: (jnp.minimum(i, split_tile - 1), 0)), wspec,
                    pl.BlockSpec((tm, k), lambda j, i: (jnp.maximum(i - split_tile, 0), 0))]
        args = [a, w, a2]
    if res is not None:
        in_specs.append(pl.BlockSpec((tm, tn), lambda j, i: (i, j)))
        args.append(res)
    return pl.pallas_call(
        functools.partial(_mm_body, has_res=res is not None, w_rows_are_outputs=False, split_tile=split_tile),
        grid=(n_cols // tn, m // tm), in_specs=in_specs,
        out_specs=pl.BlockSpec((tm, tn), lambda j, i: (i, j)),
        out_shape=jax.ShapeDtypeStruct((m, n_cols), out_dtype),
        scratch_shapes=[pltpu.VMEM((k, tn), BF16)],
        compiler_params=_params(2), name="matmul_ws")(*args)


def matmul_wt(a, wt, row_of_tile, n_cols, tm, tn, out_dtype):
    m, k = a.shape
    wspec = pl.BlockSpec((pl.Element(tn), pl.Element(k)),
                         lambda j, i: (pl.multiple_of(row_of_tile(j), SUBLANES), 0))
    return pl.pallas_call(
        functools.partial(_mm_body, has_res=False, w_rows_are_outputs=True, split_tile=None),
        grid=(n_cols // tn, m // tm),
        in_specs=[pl.BlockSpec((tm, k), lambda j, i: (i, 0)), wspec],
        out_specs=pl.BlockSpec((tm, tn), lambda j, i: (i, j)),
        out_shape=jax.ShapeDtypeStruct((m, n_cols), out_dtype),
        scratch_shapes=[pltpu.VMEM((k, tn), BF16)],
        compiler_params=_params(2), name="matmul_wt")(a, wt)


def _mix_out_body(o1_ref, o2_ref, y1_ref, y2_ref, w_ref, r1_ref, r2_ref, out_ref, wb_ref, *, split_tile):
    i = pl.program_id(1)

    @pl.when(i == 0)
    def _cast_weights():
        wb_ref[...] = w_ref[...].astype(BF16)

    half = o1_ref.shape[1]

    def tile(o_ref, y_ref, r_ref):
        acc = _dot(o_ref[...], wb_ref[0:half, :]) + _dot(y_ref[...], wb_ref[half:2 * half, :])
        out_ref[...] = acc + r_ref[...]

    pl.when(i < split_tile)(lambda: tile(o1_ref, y1_ref, r1_ref))
    pl.when(i >= split_tile)(lambda: tile(o2_ref, y2_ref, r2_ref))


def mix_out(o1, o2, y1, y2, w, lead, res1, res2, tm, tn):
    m1, half = o1.shape
    m = m1 + o2.shape[0]
    n = w.shape[-1]
    split_tile = m1 // tm

    def first(j, i):
        return jnp.minimum(i, split_tile - 1)

    def second(j, i):
        return jnp.maximum(i - split_tile, 0)

    return pl.pallas_call(
        functools.partial(_mix_out_body, split_tile=split_tile), grid=(n // tn, m // tm),
        in_specs=[pl.BlockSpec((tm, half), lambda j, i: (first(j, i), 0)),
                  pl.BlockSpec((tm, half), lambda j, i: (second(j, i), 0)),
                  pl.BlockSpec((tm, half), lambda j, i: (first(j, i), 0)),
                  pl.BlockSpec((tm, half), lambda j, i: (second(j, i), 0)),
                  pl.BlockSpec((None,) * len(lead) + (2 * half, tn), lambda j, i: tuple(lead) + (0, j)),
                  pl.BlockSpec((tm, tn), lambda j, i: (first(j, i), j)),
                  pl.BlockSpec((tm, tn), lambda j, i: (second(j, i), j))],
        out_specs=pl.BlockSpec((tm, tn), lambda j, i: (i, j)),
        out_shape=jax.ShapeDtypeStruct((m, n), F32),
        scratch_shapes=[pltpu.VMEM((2 * half, tn), BF16)],
        compiler_params=_params(2), name="mix_out")(o1, o2, y1, y2, w, res1, res2)


def _log_sigmoid(x):
    return jnp.minimum(x, 0.0) - jnp.log1p(jnp.exp(-jnp.abs(x)))


def _gla_body(*refs, chunk, n_sub, has_s0):
    q_ref, k_ref, v_ref, r_ref, g_ref, w2_ref, bg_ref, go_ref = refs[:8]
    if has_s0:
        s0_ref, o_ref, s_ref = refs[8:]
    else:
        o_ref, s_ref = refs[8:]

    @pl.when(pl.program_id(1) == 0)
    def _init_state():
        s_ref[...] = s0_ref[...] if has_s0 else jnp.zeros_like(s_ref)

    gate_in = _dot(g_ref[...].astype(BF16), w2_ref[...].astype(BF16)) + bg_ref[...]
    log_a = _log_sigmoid(gate_in) / GLA_TAU

    row = lax.broadcasted_iota(I32, (chunk, chunk), 0)
    col = lax.broadcasted_iota(I32, (chunk, chunk), 1)
    causal = row >= col
    tri = jnp.where(causal, 1.0, 0.0).astype(BF16)
    ones = jnp.ones((chunk, GLA_DV), BF16)

    seq_outs = []
    for sq in range(n_sub):
        rs = slice(sq * chunk, (sq + 1) * chunk)
        parts = _split3(log_a[rs, :])
        b_all = _dot(tri, parts[0]) + _dot(tri, parts[1]) + _dot(tri, parts[2])
        head_outs = []
        for h in range(GLA_HEADS):
            ks = slice(h * GLA_DK, (h + 1) * GLA_DK)
            vs = slice(h * GLA_DV, (h + 1) * GLA_DV)
            b = b_all[:, ks]
            b_last = b[chunk - 1:chunk, :]
            b_tot_col = (_dot_tn(parts[0][:, ks], ones) + _dot_tn(parts[1][:, ks], ones)
                         + _dot_tn(parts[2][:, ks], ones))
            q = q_ref[rs, ks] * (GLA_DK ** -0.5)
            k = k_ref[rs, ks]
            v = v_ref[rs, vs].astype(BF16)
            q_dec = (q * jnp.exp(b)).astype(BF16)
            k_inv = (k * jnp.exp(-b)).astype(BF16)
            k_end = (k * jnp.exp(b_last - b)).astype(BF16)
            att = jnp.where(causal, _dot_nt(q_dec, k_inv), 0.0).astype(BF16)
            s_old = s_ref[sq, h]
            o = _dot(q_dec, s_old.astype(BF16)) + _dot(att, v)
            s_ref[sq, h] = jnp.exp(b_tot_col) * s_old + _dot_tn(k_end, v)
            r = r_ref[rs, vs]
            head_outs.append(_rms(o, go_ref[:, vs]) * (r * jax.nn.sigmoid(r)))
        seq_outs.append(jnp.concatenate(head_outs, axis=1))
    o_ref[...] = jnp.concatenate(seq_outs, axis=0).astype(o_ref.dtype)


def gla(proj, gate_lr, w2_pad, b_gate, g_out, s0, li, n_seq, seq_len, row0, chunk, n_sub):
    nc = seq_len // chunk
    assert n_sub == 1 or nc == 1
    rows_per_step = n_sub * chunk
    rb0 = row0 // rows_per_step

    def rows(b, c):
        return rb0 + b * nc + c

    state_block = (None, n_sub, GLA_HEADS, GLA_DK, GLA_DV)
    in_specs = [
        pl.BlockSpec((rows_per_step, GLA_KW), lambda b, c: (rows(b, c), COL_Q // GLA_KW)),
        pl.BlockSpec((rows_per_step, GLA_KW), lambda b, c: (rows(b, c), COL_K // GLA_KW)),
        pl.BlockSpec((rows_per_step, GLA_VW), lambda b, c: (rows(b, c), COL_V // GLA_VW)),
        pl.BlockSpec((rows_per_step, GLA_VW), lambda b, c: (rows(b, c), COL_R // GLA_VW)),
        pl.BlockSpec((rows_per_step, LANES), lambda b, c: (rows(b, c), 0)),
        pl.BlockSpec((LANES, GLA_KW), lambda b, c: (0, 0)),
        pl.BlockSpec((1, GLA_KW), lambda b, c: (0, 0)),
        pl.BlockSpec((1, GLA_VW), lambda b, c: (0, 0)),
    ]
    args = [proj, proj, proj, proj, gate_lr, w2_pad, b_gate, g_out]
    if s0 is not None:
        in_specs.append(pl.BlockSpec(state_block, lambda b, c: (li, b, 0, 0, 0)))
        args.append(s0)
    return pl.pallas_call(
        functools.partial(_gla_body, chunk=chunk, n_sub=n_sub, has_s0=s0 is not None),
        grid=(n_seq // n_sub, nc), in_specs=in_specs,
        out_specs=[pl.BlockSpec((rows_per_step, GLA_VW), lambda b, c: (b * nc + c, 0)),
                   pl.BlockSpec(state_block, lambda b, c: (0, b, 0, 0, 0))],
        out_shape=[jax.ShapeDtypeStruct((n_seq * seq_len, GLA_VW), BF16),
                   jax.ShapeDtypeStruct((1, n_seq, GLA_HEADS, GLA_DK, GLA_DV), F32)],
        compiler_params=_params(2), name="gla")(*args)


def _conv_body(gb_ref, gc_ref, u_ref, buf_ref, w_ref, o_ref, st_ref, sc_ref, *, seq_len, n_sub):
    w = w_ref[...]
    outs = []
    for sq in range(n_sub):
        rs = slice(sq * seq_len, (sq + 1) * seq_len)
        cu = gc_ref[rs, :] * u_ref[rs, :]
        sc_ref[0:8, :] = jnp.zeros((8, cu.shape[1]), F32)
        sc_ref[6:8, :] = buf_ref[sq]
        sc_ref[8:8 + seq_len, :] = cu
        z = sc_ref[6:6 + seq_len, :] * w[0:1, :] + sc_ref[7:7 + seq_len, :] * w[1:2, :] + cu * w[2:3, :]
        outs.append(gb_ref[rs, :] * z)
        st_ref[sq] = sc_ref[6 + seq_len:8 + seq_len, :]
    o_ref[...] = jnp.concatenate(outs, axis=0).astype(o_ref.dtype)


def short_conv(proj, buf, conv_w, li_buf, li_w, n_seq, seq_len, row0, ct, n_sub):
    rows_per_step = n_sub * seq_len
    rb0 = row0 // rows_per_step
    state_block = (None, n_sub, CONV_WIDTH - 1, ct)
    return pl.pallas_call(
        functools.partial(_conv_body, seq_len=seq_len, n_sub=n_sub),
        grid=(n_seq // n_sub, CONV_CH // ct),
        in_specs=[
            pl.BlockSpec((rows_per_step, ct), lambda b, j: (rb0 + b, COL_B // ct + j)),
            pl.BlockSpec((rows_per_step, ct), lambda b, j: (rb0 + b, COL_C // ct + j)),
            pl.BlockSpec((rows_per_step, ct), lambda b, j: (rb0 + b, COL_U // ct + j)),
            pl.BlockSpec(state_block, lambda b, j: (li_buf, b, 0, j)),
            pl.BlockSpec((None, CONV_WIDTH, ct), lambda b, j: (li_w, 0, j)),
        ],
        out_specs=[pl.BlockSpec((rows_per_step, ct), lambda b, j: (b, j)),
                   pl.BlockSpec(state_block, lambda b, j: (0, b, 0, j))],
        out_shape=[jax.ShapeDtypeStruct((n_seq * seq_len, CONV_CH), BF16),
                   jax.ShapeDtypeStruct((1, n_seq, CONV_WIDTH - 1, CONV_CH), F32)],
        scratch_shapes=[pltpu.VMEM((seq_len + 8, ct), F32)],
        compiler_params=_params(2), name="short_conv")(proj, proj, proj, buf, conv_w)


def _swap_halves(x, lane):
    return jnp.where(lane % ROPE < ROPE // 2, pltpu.roll(x, LANES - ROPE // 2, 1), pltpu.roll(x, ROPE // 2, 1))


def _mla_post_body(cq_ref, ckv_ref, kpe_ref, cos_ref, sin_ref, gq_ref, gkv_ref,
                   cqn_ref, lat_ref, latb_ref, pe_ref, peb_ref):
    cqn_ref[...] = _rms(cq_ref[...], gq_ref[...]).astype(BF16)
    lat = _rms(ckv_ref[...], gkv_ref[...])
    lat_ref[...] = lat
    latb_ref[...] = lat.astype(BF16)
    kpe = kpe_ref[...]
    lane = lax.broadcasted_iota(I32, kpe.shape, 1)
    pe = kpe * cos_ref[...] + _swap_halves(kpe, lane) * sin_ref[...]
    pe_ref[...] = pe
    peb_ref[...] = pe.astype(BF16)


def mla_post(c, kpe_raw, cos, sin, g_q, g_kv, tm=256):
    t = c.shape[0]
    return pl.pallas_call(
        _mla_post_body, grid=(t // tm,),
        in_specs=[
            pl.BlockSpec((tm, Q_LORA), lambda i: (i, 0)),
            pl.BlockSpec((tm, KV_LORA), lambda i: (i, Q_LORA // KV_LORA)),
            pl.BlockSpec((tm, LANES), lambda i: (i, 0)),
            pl.BlockSpec((tm, LANES), lambda i: (i, 0)),
            pl.BlockSpec((tm, LANES), lambda i: (i, 0)),
            pl.BlockSpec((1, Q_LORA), lambda i: (0, 0)),
            pl.BlockSpec((1, KV_LORA), lambda i: (0, 0)),
        ],
        out_specs=[
            pl.BlockSpec((tm, Q_LORA), lambda i: (i, 0)),
            pl.BlockSpec((tm, KV_LORA), lambda i: (i, 0)),
            pl.BlockSpec((tm, KV_LORA), lambda i: (i, 0)),
            pl.BlockSpec((tm, LANES), lambda i: (i, 0)),
            pl.BlockSpec((tm, LANES), lambda i: (i, 0)),
        ],
        out_shape=[
            jax.ShapeDtypeStruct((t, Q_LORA), BF16),
            jax.ShapeDtypeStruct((t, KV_LORA), F32),
            jax.ShapeDtypeStruct((t, KV_LORA), BF16),
            jax.ShapeDtypeStruct((t, LANES), F32),
            jax.ShapeDtypeStruct((t, LANES), BF16),
        ],
        compiler_params=_params(1), name="mla_post")(c, c, kpe_raw, cos, sin, g_q, g_kv)


def _q_proj_body(a_ref, w_ref, cos_ref, sin_ref, o_ref, wb_ref, *, n_nope_tiles):
    j = pl.program_id(0)

    @pl.when(pl.program_id(1) == 0)
    def _cast_weights():
        wb_ref[...] = w_ref[...].astype(BF16)

    acc = _dot(a_ref[...], wb_ref[...])

    @pl.when(j < n_nope_tiles)
    def _nope():
        o_ref[...] = (acc * MLA_SCALE).astype(BF16)

    @pl.when(j >= n_nope_tiles)
    def _rope():
        cos, sin = cos_ref[...], sin_ref[...]
        lane = lax.broadcasted_iota(I32, cos.shape, 1)
        for c in range(acc.shape[1] // LANES):
            x = acc[:, c * LANES:(c + 1) * LANES]
            y = x * cos + _swap_halves(x, lane) * sin
            o_ref[:, c * LANES:(c + 1) * LANES] = (y * MLA_SCALE).astype(BF16)


def q_proj(cqn, w_uq_re, cos, sin, tm=1024, tn=1024):
    m, k = cqn.shape
    n = w_uq_re.shape[1]
    return pl.pallas_call(
        functools.partial(_q_proj_body, n_nope_tiles=MLA_HEADS * NOPE // tn), grid=(n // tn, m // tm),
        in_specs=[pl.BlockSpec((tm, k), lambda j, i: (i, 0)),
                  pl.BlockSpec((k, tn), lambda j, i: (0, j)),
                  pl.BlockSpec((tm, LANES), lambda j, i: (i, 0)),
                  pl.BlockSpec((tm, LANES), lambda j, i: (i, 0))],
        out_specs=pl.BlockSpec((tm, tn), lambda j, i: (i, j)),
        out_shape=jax.ShapeDtypeStruct((m, n), BF16),
        scratch_shapes=[pltpu.VMEM((k, tn), BF16)],
        compiler_params=_params(2), name="q_proj")(cqn, w_uq_re, cos, sin)


def _flash_body(qn_ref, qp_ref, kn_ref, kp_ref, v_ref, o_ref, kcat_ref, *, tq):
    qi = pl.program_id(2)
    heads = [slice(h * NOPE, (h + 1) * NOPE) for h in range(2)]

    @pl.when(qi == 0)
    def _build_keys():
        for h, hs in enumerate(heads):
            kcat_ref[h, :, 0:NOPE] = kn_ref[:, hs]
            kcat_ref[h, :, NOPE:2 * NOPE] = kp_ref[...]

    q = [jnp.concatenate([qn_ref[:, hs], qp_ref[:, hs]], axis=1) for hs in heads]
    on_or_below_diag = (lax.broadcasted_iota(I32, (tq, tq), 0) >= lax.broadcasted_iota(I32, (tq, tq), 1))

    def block(j, carry, diagonal):
        ks = pl.ds(pl.multiple_of(j * tq, tq), tq)
        out = []
        for h, hs in enumerate(heads):
            m, l, acc = carry[h]
            s = _dot_nt(q[h], kcat_ref[h, ks, :])
            if diagonal:
                s = jnp.where(on_or_below_diag, s, -jnp.inf)
            m_new = jnp.maximum(m, jnp.max(s, axis=1, keepdims=True))
            alpha = jnp.exp(m - m_new)
            p = jnp.exp(s - m_new)
            l = alpha * l + jnp.sum(p, axis=1, keepdims=True)
            acc = alpha * acc + _dot(p.astype(BF16), v_ref[ks, hs])
            out.append((m_new, l, acc))
        return tuple(out)

    init = tuple((jnp.full((tq, 1), -jnp.inf, F32), jnp.zeros((tq, 1), F32), jnp.zeros((tq, V_DIM), F32))
                 for _ in heads)
    carry = lax.fori_loop(0, qi, lambda j, c: block(j, c, False), init)
    carry = block(qi, carry, True)
    for h, hs in enumerate(heads):
        _, l, acc = carry[h]
        o_ref[:, hs] = (acc / l).astype(o_ref.dtype)


def flash_prompt(qs, kn, kp, v, n_seq, seq_len, tq=512):
    nq = seq_len // tq
    hw = 2 * NOPE
    rope_cb0 = MLA_HEADS * NOPE // hw
    return pl.pallas_call(
        functools.partial(_flash_body, tq=tq),
        grid=(n_seq, MLA_HEADS // 2, nq),
        in_specs=[
            pl.BlockSpec((tq, hw), lambda b, h, i: (b * nq + i, h)),
            pl.BlockSpec((tq, hw), lambda b, h, i: (b * nq + i, rope_cb0 + h)),
            pl.BlockSpec((seq_len, hw), lambda b, h, i: (b, h)),
            pl.BlockSpec((seq_len, LANES), lambda b, h, i: (b, 0)),
            pl.BlockSpec((seq_len, hw), lambda b, h, i: (b, h)),
        ],
        out_specs=pl.BlockSpec((tq, hw), lambda b, h, i: (b * nq + i, h)),
        out_shape=jax.ShapeDtypeStruct((n_seq * seq_len, MLA_HEADS * V_DIM), BF16),
        scratch_shapes=[pltpu.VMEM((2, seq_len, 2 * NOPE), BF16)],
        compiler_params=_params(3), name="flash_prompt")(qs, qs, kn, kp, v)


def _absorb_q_body(qn_ref, qp_ref, w_ref, ql_ref, qpo_ref, *, n_seq, dec_seq):
    w = w_ref[...].astype(BF16)
    ql_ref[...] = _dot_nt(qn_ref[...], w).reshape(n_seq, dec_seq, KV_LORA)
    qpo_ref[...] = qp_ref[...].astype(F32).reshape(n_seq, dec_seq, LANES)


def absorb_q(qs, w_uk, li, row0, n_seq, dec_seq):
    n_rows = n_seq * dec_seq
    rope_cb0 = MLA_HEADS * NOPE // LANES
    return pl.pallas_call(
        functools.partial(_absorb_q_body, n_seq=n_seq, dec_seq=dec_seq), grid=(MLA_HEADS,),
        in_specs=[pl.BlockSpec((n_rows, NOPE), lambda h: (row0 // n_rows, h)),
                  pl.BlockSpec((n_rows, LANES), lambda h: (row0 // n_rows, rope_cb0 + h)),
                  pl.BlockSpec((None, KV_LORA, NOPE), lambda h: (li, 0, h))],
        out_specs=[pl.BlockSpec((n_seq, dec_seq, KV_LORA), lambda h: (0, h, 0)),
                   pl.BlockSpec((n_seq, dec_seq, LANES), lambda h: (0, h, 0))],
        out_shape=[jax.ShapeDtypeStruct((n_seq, MLA_HEADS * dec_seq, KV_LORA), F32),
                   jax.ShapeDtypeStruct((n_seq, MLA_HEADS * dec_seq, LANES), F32)],
        compiler_params=_params(1), name="absorb_q")(qs, qs, w_uk)


def _expand_v_body(a_ref, w_ref, o_ref):
    n_seq, dec_seq, _ = a_ref.shape
    a = a_ref[...].reshape(n_seq * dec_seq, KV_LORA).astype(BF16)
    o_ref[...] = _dot(a, w_ref[...].astype(BF16)).astype(o_ref.dtype)


def expand_v(o_lat, w_uv, li, dec_seq):
    n_seq = o_lat.shape[0]
    n_rows = n_seq * dec_seq
    return pl.pallas_call(
        _expand_v_body, grid=(MLA_HEADS,),
        in_specs=[pl.BlockSpec((n_seq, dec_seq, KV_LORA), lambda h: (0, h, 0)),
                  pl.BlockSpec((None, KV_LORA, V_DIM), lambda h: (li, 0, h))],
        out_specs=pl.BlockSpec((n_rows, V_DIM), lambda h: (0, h)),
        out_shape=jax.ShapeDtypeStruct((n_rows, MLA_HEADS * V_DIM), BF16),
        compiler_params=_params(1), name="expand_v")(o_lat, w_uv)


def _decode_body(pt_ref, ql_ref, qp_ref, latn_ref, pen_ref, lat_hbm, pe_hbm, o_ref,
                 lat_buf, pe_buf, sem, qlb_ref, qpb_ref, m_ref, l_ref, acc_ref, *, li, n_pages, dec_seq):
    s_idx = pl.program_id(1)
    n_steps = pl.num_programs(1)
    step = pl.program_id(0) * n_steps + s_idx
    n_rows = ql_ref.shape[0]

    def page_copies(seq, kv_step, slot):
        out = []
        for kk in range(n_pages):
            page = pt_ref[seq, kv_step * n_pages + kk]
            keys = pl.ds(kk * PAGE_SIZE, PAGE_SIZE)
            out.append(pltpu.make_async_copy(lat_hbm.at[li, page], lat_buf.at[slot, keys], sem.at[0, slot]))
            out.append(pltpu.make_async_copy(pe_hbm.at[li, page], pe_buf.at[slot, :, keys], sem.at[1, slot]))
        return out

    @pl.when(step == 0)
    def _first():
        for cp in page_copies(0, 0, 0):
            cp.start()

    @pl.when(step + 1 < pl.num_programs(0) * n_steps)
    def _prefetch():
        nxt = step + 1
        for cp in page_copies(nxt // n_steps, nxt % n_steps, nxt % 2):
            cp.start()

    @pl.when(s_idx == 0)
    def _init():
        qlb_ref[...] = ql_ref[...].astype(BF16)
        qpb_ref[...] = qp_ref[...].astype(BF16)
        m_ref[...] = jnp.full(m_ref.shape, -jnp.inf, F32)
        l_ref[...] = jnp.zeros(l_ref.shape, F32)
        acc_ref[...] = jnp.zeros(acc_ref.shape, F32)

    ql = qlb_ref[...]
    qp = qpb_ref[:, :ROPE]

    def update(scores, keys):
        m_prev = m_ref[...]
        m_new = m_prev
        for s in scores:
            m_new = jnp.maximum(m_new, jnp.max(s, axis=1, keepdims=True))
        alpha = jnp.exp(m_prev - m_new)
        l_new = alpha * l_ref[...]
        acc = alpha * acc_ref[...]
        for s, kc in zip(scores, keys):
            p = jnp.exp(s - m_new)
            l_new = l_new + jnp.sum(p, axis=1, keepdims=True)
            acc = acc + _dot(p.astype(BF16), kc)
        m_ref[...] = m_new
        l_ref[...] = l_new
        acc_ref[...] = acc

    slot = step % 2
    for cp in page_copies(pl.program_id(0), s_idx, slot):
        cp.wait()
    lat_slot = lat_buf.at[slot]
    pe_slot = pe_buf.at[slot]
    scores, keys = [], []
    for c in range(n_pages * PAGE_SIZE // DECODE_CHUNK):
        cs = slice(c * DECODE_CHUNK, (c + 1) * DECODE_CHUNK)
        kc = lat_slot[cs, :].astype(BF16)
        keys.append(kc)
        scores.append(_dot_nt(ql, kc) + _dot(qp, pe_slot[:, cs].astype(BF16)))
    for c0 in range(0, len(scores), DECODE_UPDATE_CHUNKS):
        update(scores[c0:c0 + DECODE_UPDATE_CHUNKS], keys[c0:c0 + DECODE_UPDATE_CHUNKS])

    @pl.when(s_idx == n_steps - 1)
    def _new_tokens():
        pad = PAGE_SIZE - dec_seq
        lat_new = jnp.concatenate([latn_ref[...], jnp.zeros((pad, KV_LORA), F32)], axis=0).astype(BF16)
        pe_new = jnp.concatenate([pen_ref[:, :ROPE], jnp.zeros((pad, ROPE), F32)], axis=0).astype(BF16)
        s = _dot_nt(ql, lat_new) + _dot_nt(qp, pe_new)
        q_tok = lax.broadcasted_iota(I32, (n_rows, PAGE_SIZE), 0) % dec_seq
        key_tok = lax.broadcasted_iota(I32, (n_rows, PAGE_SIZE), 1)
        update([jnp.where(key_tok <= q_tok, s, -jnp.inf)], [lat_new])
        o_ref[...] = acc_ref[...] / l_ref[...]


def decode_attention(page_table, q_lat, q_pe, cache_lat, cache_pe_t, lat_new, pe_new, li, row0, dec_seq):
    n_seq, n_rows, _ = q_lat.shape
    n_steps = page_table.shape[1] // PAGES_PER_STEP
    n_keys = PAGES_PER_STEP * PAGE_SIZE
    grid_spec = pltpu.PrefetchScalarGridSpec(
        num_scalar_prefetch=1, grid=(n_seq, n_steps),
        in_specs=[pl.BlockSpec((None, n_rows, KV_LORA), lambda b, s, pt: (b, 0, 0)),
                  pl.BlockSpec((None, n_rows, LANES), lambda b, s, pt: (b, 0, 0)),
                  pl.BlockSpec((dec_seq, KV_LORA), lambda b, s, pt: (row0 // dec_seq + b, 0)),
                  pl.BlockSpec((dec_seq, LANES), lambda b, s, pt: (row0 // dec_seq + b, 0)),
                  pl.BlockSpec(memory_space=pl.ANY), pl.BlockSpec(memory_space=pl.ANY)],
        out_specs=pl.BlockSpec((None, n_rows, KV_LORA), lambda b, s, pt: (b, 0, 0)),
        scratch_shapes=[pltpu.VMEM((2, n_keys, KV_LORA), F32), pltpu.VMEM((2, ROPE, n_keys), F32),
                        pltpu.SemaphoreType.DMA((2, 2)),
                        pltpu.VMEM((n_rows, KV_LORA), BF16), pltpu.VMEM((n_rows, LANES), BF16),
                        pltpu.VMEM((n_rows, 1), F32), pltpu.VMEM((n_rows, 1), F32),
                        pltpu.VMEM((n_rows, KV_LORA), F32)])
    return pl.pallas_call(
        functools.partial(_decode_body, li=li, n_pages=PAGES_PER_STEP, dec_seq=dec_seq),
        grid_spec=grid_spec, out_shape=jax.ShapeDtypeStruct((n_seq, n_rows, KV_LORA), F32),
        compiler_params=_params(2), name="decode_attention")(
            page_table, q_lat, q_pe, lat_new, pe_new, cache_lat, cache_pe_t)


def _router_body(x_ref, g_ref, w_ref, b_ref, o_ref):
    h = _rms(x_ref[...], g_ref[...])
    logits = jnp.dot(h, w_ref[...], preferred_element_type=F32, precision=lax.Precision.HIGHEST) + b_ref[...]
    lane = lax.broadcasted_iota(I32, logits.shape, 1)
    lane_f = lane.astype(F32)
    neg = -jnp.inf
    lg = jnp.where(lane < N_GROUPS, logits, neg)
    mg = jnp.max(lg, axis=1, keepdims=True)
    grp = jnp.min(jnp.where(lg == mg, lane_f, float(LANES)), axis=1, keepdims=True)
    p_grp = 1.0 / jnp.sum(jnp.exp(lg - mg), axis=1, keepdims=True)
    e_lane = lane_f - float(N_GROUPS)
    in_grp = (lane >= N_GROUPS) & (lane < N_GROUPS + N_EXPERTS) & (jnp.floor(e_lane / EXP_PER_GROUP) == grp)
    le = jnp.where(in_grp, logits, neg)
    m1 = jnp.max(le, axis=1, keepdims=True)
    i1 = jnp.min(jnp.where(le == m1, lane_f, float(LANES)), axis=1, keepdims=True)
    le2 = jnp.where(lane_f == i1, neg, le)
    m2 = jnp.max(le2, axis=1, keepdims=True)
    i2 = jnp.min(jnp.where(le2 == m2, lane_f, float(LANES)), axis=1, keepdims=True)
    e2 = jnp.exp(m2 - m1)
    g1 = p_grp / (1.0 + e2)
    g2 = p_grp * e2 / (1.0 + e2)
    out = jnp.where(lane == 0, i1 - N_GROUPS, jnp.where(lane == 1, i2 - N_GROUPS,
                    jnp.where(lane == 2, g1, jnp.where(lane == 3, g2, 0.0))))
    o_ref[...] = out


def moe_router(x, g, w_router, b_router, tm=256):
    t, d = x.shape
    return pl.pallas_call(
        _router_body, grid=(t // tm,),
        in_specs=[pl.BlockSpec((tm, d), lambda i: (i, 0)), pl.BlockSpec((1, d), lambda i: (0, 0)),
                  pl.BlockSpec((d, LANES), lambda i: (0, 0)), pl.BlockSpec((1, LANES), lambda i: (0, 0))],
        out_specs=pl.BlockSpec((tm, LANES), lambda i: (i, 0)),
        out_shape=jax.ShapeDtypeStruct((t, LANES), F32),
        compiler_params=_params(1), name="moe_router")(x, g.reshape(1, d), w_router, b_router)


def _row_copy(src_hbm, dst, sem, src_row, dst_row):
    return pltpu.make_async_copy(src_hbm.at[pl.ds(src_row, 1)], dst.at[pl.ds(dst_row, 1)], sem)


def _rows_done(src_hbm, dst, sem):
    return pltpu.make_async_copy(src_hbm.at[pl.ds(0, dst.shape[0])], dst, sem)


def _dispatch_body(tok_ref, nsteps_ref, x_hbm, g_ref, o_ref, buf, sem, *, tb):
    i = pl.program_id(0)
    n_used = nsteps_ref[0]

    def start_all(step, slot):
        def body(r, c):
            _row_copy(x_hbm, buf.at[slot], sem.at[slot], tok_ref[step * tb + r], r).start()
            return c
        lax.fori_loop(0, tb, body, 0, unroll=GATHER_UNROLL)

    @pl.when(i == 0)
    def _first():
        start_all(0, 0)

    @pl.when(i + 1 < n_used)
    def _prefetch():
        start_all(i + 1, (i + 1) % 2)

    @pl.when(i < n_used)
    def _compute():
        slot = i % 2
        _rows_done(x_hbm, buf.at[slot], sem.at[slot]).wait()
        o_ref[...] = _rms(buf[slot], g_ref[...]).astype(o_ref.dtype)

    @pl.when(i >= n_used)
    def _unused_block():
        o_ref[...] = jnp.zeros_like(o_ref)


def moe_dispatch(tok_sorted, n_used_steps, x, g, tb=GATHER_TB):
    n_rows = tok_sorted.shape[0]
    d = x.shape[1]
    grid_spec = pltpu.PrefetchScalarGridSpec(
        num_scalar_prefetch=2, grid=(n_rows // tb,),
        in_specs=[pl.BlockSpec(memory_space=pl.ANY), pl.BlockSpec((1, d), lambda i, tok, ns: (0, 0))],
        out_specs=pl.BlockSpec((tb, d), lambda i, tok, ns: (i, 0)),
        scratch_shapes=[pltpu.VMEM((2, tb, d), F32), pltpu.SemaphoreType.DMA((2,))])
    return pl.pallas_call(
        functools.partial(_dispatch_body, tb=tb), grid_spec=grid_spec,
        out_shape=jax.ShapeDtypeStruct((n_rows, d), BF16),
        compiler_params=_params(1), name="moe_dispatch")(tok_sorted, n_used_steps, x, g.reshape(1, d))


def _weight_tile_stream(tabs, t, hbm_tiles, bufs, sem):
    e_ref, cw_ref, slot_ref, ne_ref, ncw_ref, has_next_ref = tabs

    def copies(e, cw, slot):
        return [pltpu.make_async_copy(src, buf.at[slot], sem.at[n, slot])
                for n, (src, buf) in enumerate(zip(hbm_tiles(e, cw), bufs))]

    slot = slot_ref[t]

    @pl.when(t == 0)
    def _first_group():
        for cp in copies(e_ref[0], cw_ref[0], 0):
            cp.start()

    @pl.when(has_next_ref[t] == 1)
    def _next_group():
        for cp in copies(ne_ref[t], ncw_ref[t], 1 - slot):
            cp.start()

    for cp in copies(e_ref[t], cw_ref[t], slot):
        cp.wait()
    return slot


def _moe_up_body(e_ref, cw_ref, rb_ref, co_ref, first_ref, valid_ref, slot_ref, ne_ref, ncw_ref, hn_ref,
                 x_ref, wg_hbm, wu_hbm, o_ref, wg_buf, wu_buf, sem, wgb, wub, *, layer):
    t = pl.program_id(0)

    @pl.when(first_ref[t] == 1)
    def _new_weights():
        def tiles(e, cw):
            cols = pl.ds(pl.multiple_of(cw * MOE_TF, MOE_TF), MOE_TF)
            return [wg_hbm.at[layer, e, :, cols], wu_hbm.at[layer, e, :, cols]]
        slot = _weight_tile_stream((e_ref, cw_ref, slot_ref, ne_ref, ncw_ref, hn_ref), t, tiles,
                                   (wg_buf, wu_buf), sem)
        wgb[...] = wg_buf[slot].astype(BF16)
        wub[...] = wu_buf[slot].astype(BF16)

    @pl.when(valid_ref[t] == 1)
    def _compute():
        x = x_ref[...]
        a = _dot(x, wgb[...])
        u = _dot(x, wub[...])
        o_ref[...] = (a * jax.nn.sigmoid(a) * u).astype(o_ref.dtype)

    @pl.when(valid_ref[t] == 0)
    def _unused_block():
        o_ref[...] = jnp.zeros_like(o_ref)


def moe_up(items, xs, w_gate, w_up, layer):
    n_rows, d = xs.shape
    n_items = items[0].shape[0]
    n_tabs = len(items)
    grid_spec = pltpu.PrefetchScalarGridSpec(
        num_scalar_prefetch=n_tabs, grid=(n_items,),
        in_specs=[pl.BlockSpec((MOE_TM, d), lambda t, *tabs: (tabs[2][t], 0)),
                  pl.BlockSpec(memory_space=pl.ANY), pl.BlockSpec(memory_space=pl.ANY)],
        out_specs=pl.BlockSpec((MOE_TM, MOE_TF), lambda t, *tabs: (tabs[2][t], tabs[3][t])),
        scratch_shapes=[pltpu.VMEM((2, d, MOE_TF), F32), pltpu.VMEM((2, d, MOE_TF), F32),
                        pltpu.SemaphoreType.DMA((2, 2)),
                        pltpu.VMEM((d, MOE_TF), BF16), pltpu.VMEM((d, MOE_TF), BF16)])
    return pl.pallas_call(
        functools.partial(_moe_up_body, layer=layer), grid_spec=grid_spec,
        out_shape=jax.ShapeDtypeStruct((n_rows, D_EXPERT), BF16),
        compiler_params=_params(1), name="moe_up")(*items, xs, w_gate, w_up)


def _moe_down_body(e_ref, cw_ref, rb_ref, co_ref, first_ref, valid_ref, slot_ref, ne_ref, ncw_ref, hn_ref,
                   h_ref, wd_hbm, o_ref, wd_buf, sem, wdb, *, layer):
    t = pl.program_id(0)

    @pl.when(first_ref[t] == 1)
    def _new_weights():
        def tiles(e, cw):
            return [wd_hbm.at[layer, e, :, pl.ds(pl.multiple_of(cw * MOE_TN, MOE_TN), MOE_TN)]]
        slot = _weight_tile_stream((e_ref, cw_ref, slot_ref, ne_ref, ncw_ref, hn_ref), t, tiles, (wd_buf,), sem)
        wdb[...] = wd_buf[slot].astype(BF16)

    @pl.when(valid_ref[t] == 1)
    def _compute():
        o_ref[...] = _dot(h_ref[...], wdb[...]).astype(o_ref.dtype)

    @pl.when(valid_ref[t] == 0)
    def _unused_block():
        o_ref[...] = jnp.zeros_like(o_ref)


def moe_down(items, hmid, w_down, layer):
    n_rows, f = hmid.shape
    d = w_down.shape[-1]
    n_items = items[0].shape[0]
    grid_spec = pltpu.PrefetchScalarGridSpec(
        num_scalar_prefetch=len(items), grid=(n_items,),
        in_specs=[pl.BlockSpec((MOE_TM, f), lambda t, *tabs: (tabs[2][t], 0)),
                  pl.BlockSpec(memory_space=pl.ANY)],
        out_specs=pl.BlockSpec((MOE_TM, MOE_TN), lambda t, *tabs: (tabs[2][t], tabs[3][t])),
        scratch_shapes=[pltpu.VMEM((2, f, MOE_TN), F32), pltpu.SemaphoreType.DMA((1, 2)),
                        pltpu.VMEM((f, MOE_TN), BF16)])
    return pl.pallas_call(
        functools.partial(_moe_down_body, layer=layer), grid_spec=grid_spec,
        out_shape=jax.ShapeDtypeStruct((n_rows, d), F32),
        compiler_params=_params(1), name="moe_down")(*items, hmid, w_down)


def _combine_body(pos_ref, x_ref, gate_ref, gn_ref, yb_hbm, o1_ref, o2_ref, buf, sem, *, tb, final, split_step):
    i = pl.program_id(0)
    n = pl.num_programs(0)

    def start_all(step, slot):
        def body(r, c):
            for kk in range(TOP_K):
                _row_copy(yb_hbm, buf.at[slot, kk], sem.at[slot], pos_ref[(step * tb + r) * TOP_K + kk], r).start()
            return c
        lax.fori_loop(0, tb, body, 0, unroll=GATHER_UNROLL)

    @pl.when(i == 0)
    def _first():
        start_all(0, 0)

    @pl.when(i + 1 < n)
    def _prefetch():
        start_all(i + 1, (i + 1) % 2)

    slot = i % 2
    for kk in range(TOP_K):
        _rows_done(yb_hbm, buf.at[slot, kk], sem.at[slot]).wait()
    gate = gate_ref[...]
    y = x_ref[...] + (gate[:, 0:1] * buf[slot, 0] + gate[:, 1:2] * buf[slot, 1])
    if final:
        yn = _rms(y, gn_ref[...])

        @pl.when(i < split_step)
        def _prompt_rows():
            o1_ref[...] = yn

        @pl.when(i >= split_step)
        def _sample_rows():
            o2_ref[...] = yn
    else:
        o1_ref[...] = y
        o2_ref[...] = _rms(y, gn_ref[...]).astype(o2_ref.dtype)


def moe_combine(pos, x, gates, yb, g_norm, final, n_prompt_rows, tb=GATHER_TB):
    t, d = x.shape
    split_step = n_prompt_rows // tb
    if final:
        out_specs = [pl.BlockSpec((tb, d), lambda i, p: (jnp.minimum(i, split_step - 1), 0)),
                     pl.BlockSpec((tb, d), lambda i, p: (jnp.maximum(i - split_step, 0), 0))]
        out_shape = [jax.ShapeDtypeStruct((n_prompt_rows, d), F32),
                     jax.ShapeDtypeStruct((t - n_prompt_rows, d), F32)]
    else:
        out_specs = [pl.BlockSpec((tb, d), lambda i, p: (i, 0)), pl.BlockSpec((tb, d), lambda i, p: (i, 0))]
        out_shape = [jax.ShapeDtypeStruct((t, d), F32), jax.ShapeDtypeStruct((t, d), BF16)]
    grid_spec = pltpu.PrefetchScalarGridSpec(
        num_scalar_prefetch=1, grid=(t // tb,),
        in_specs=[pl.BlockSpec((tb, d), lambda i, p: (i, 0)), pl.BlockSpec((tb, TOP_K), lambda i, p: (i, 0)),
                  pl.BlockSpec((1, d), lambda i, p: (0, 0)), pl.BlockSpec(memory_space=pl.ANY)],
        out_specs=out_specs,
        scratch_shapes=[pltpu.VMEM((2, TOP_K, tb, d), F32), pltpu.SemaphoreType.DMA((2,))])
    return pl.pallas_call(
        functools.partial(_combine_body, tb=tb, final=final, split_step=split_step), grid_spec=grid_spec,
        out_shape=out_shape, compiler_params=_params(1), name="moe_combine")(
            pos, x, gates, g_norm.reshape(1, d), yb)


def _work_items(n_col_tiles, blocks_per_expert, block_start, n_used, n_blocks):
    n_items = n_col_tiles * n_blocks
    idx = jnp.arange(n_items, dtype=I32)
    n_valid = n_col_tiles * n_used
    valid = idx < n_valid
    t = jnp.minimum(idx, n_valid - 1)
    item_end = n_col_tiles * jnp.cumsum(blocks_per_expert)
    e = jnp.minimum(jnp.sum((t[:, None] >= item_end[None, :]).astype(I32), axis=1), N_EXPERTS - 1)
    nb = jnp.maximum(blocks_per_expert[e], 1)
    local = t - n_col_tiles * block_start[e]
    cw = (local // nb).astype(I32)
    r = local % nb
    spare = idx - n_valid
    rb = jnp.where(valid, block_start[e] + r, n_used + spare // n_col_tiles).astype(I32)
    co = jnp.where(valid, cw, spare % n_col_tiles).astype(I32)
    first = (r == 0) & valid
    slot = ((jnp.cumsum(first.astype(I32)) - 1) % 2).astype(I32)
    nxt = idx + nb
    has_next = first & (nxt < n_valid)
    nxt = jnp.minimum(nxt, n_items - 1)
    return (e, cw, rb, co, first.astype(I32), valid.astype(I32), slot, e[nxt], cw[nxt], has_next.astype(I32))


def moe_layer(x, layer, g_norm, w_group, b_group, w_expert, b_expert, w_gate, w_up, w_down,
              g_next, final, n_prompt_rows):
    t, d = x.shape
    pad = LANES - N_GROUPS - N_EXPERTS
    w_router = jnp.concatenate([w_group[layer], w_expert[layer], jnp.zeros((d, pad), F32)], axis=1)
    b_router = jnp.concatenate([b_group[layer], b_expert[layer], jnp.zeros((pad,), F32)]).reshape(1, LANES)
    routed = moe_router(x, g_norm, w_router, b_router)
    eid = routed[:, 0:TOP_K].astype(I32).reshape(-1)
    gates = routed[:, TOP_K:2 * TOP_K]

    n_assign = t * TOP_K
    onehot = (eid[:, None] == jnp.arange(N_EXPERTS, dtype=I32)[None, :]).astype(I32)
    rank = jnp.sum((jnp.cumsum(onehot, axis=0) - onehot) * onehot, axis=1)
    counts = jnp.sum(onehot, axis=0)
    blocks_per_expert = (counts + MOE_TM - 1) // MOE_TM
    block_start = jnp.cumsum(blocks_per_expert) - blocks_per_expert
    n_used = jnp.sum(blocks_per_expert)
    n_blocks = -(-(n_assign + N_EXPERTS * (MOE_TM - 1)) // MOE_TM)
    pos = (block_start[eid] * MOE_TM + rank).astype(I32)
    tok = jnp.arange(n_assign, dtype=I32) // TOP_K
    tok_sorted = jnp.zeros((n_blocks * MOE_TM,), I32).at[pos].set(tok)

    n_used_steps = (n_used * (MOE_TM // GATHER_TB)).astype(I32).reshape(1)
    xs = moe_dispatch(tok_sorted, n_used_steps, x, g_norm)
    hmid = moe_up(_work_items(D_EXPERT // MOE_TF, blocks_per_expert, block_start, n_used, n_blocks),
                  xs, w_gate, w_up, layer)
    yb = moe_down(_work_items(d // MOE_TN, blocks_per_expert, block_start, n_used, n_blocks),
                  hmid, w_down, layer)
    return moe_combine(pos, x, gates, yb, g_next, final, n_prompt_rows)


def _rope_tables(pos):
    inv = ROPE_THETA ** (-jnp.arange(0, ROPE, 2, dtype=F32) / ROPE)
    ang = pos.astype(F32)[:, None] * inv[None, :]
    cos, sin = jnp.cos(ang), jnp.sin(ang)
    return jnp.concatenate([cos] * 4, axis=1), jnp.concatenate([-sin, sin, -sin, sin], axis=1)


def _pad_rows(w, n_rows):
    return jnp.pad(w, ((0, n_rows - w.shape[0]), (0, 0)))


def kernel(x_prompt, x_sample, state_gla, state_conv, cache_kv_latent, cache_k_rope, page_table, g_mix_norm, g_ffn_norm, g_final_norm, ab_w_in, gla_w_gate2, gla_b_gate, gla_g_out, conv_w, ab_w_out, mla_w_in, mla_g_q, mla_g_kv, mla_w_uq, mla_w_uk, mla_w_uv, mla_w_out, moe_w_group, moe_b_group, moe_w_expert, moe_b_expert, moe_w_gate, moe_w_up, moe_w_down):
    n_p, len_p, d = x_prompt.shape
    n_s, len_s, _ = x_sample.shape
    t_p, t_s = n_p * len_p, n_s * len_s
    past = page_table.shape[1] * PAGE_SIZE
    x_p, x_s = x_prompt.reshape(t_p, d), x_sample.reshape(t_s, d)
    moe = (moe_w_group, moe_b_group, moe_w_expert, moe_b_expert, moe_w_gate, moe_w_up, moe_w_down)

    h = rmsnorm_stacked(x_p, x_s, g_mix_norm[0], BF16)
    w_in_t = jnp.swapaxes(ab_w_in[0], 0, 1)
    tn = 512
    proj = matmul_wt(h, w_in_t, lambda j: jnp.where(j * tn < AB_GATE_COL0, j * tn, j * tn + GLA_GATE_RANK),
                     AB_MAIN_COLS, 512, tn, F32)
    w_lr_t = _pad_rows(w_in_t[AB_GATE_COL0:AB_GATE_COL0 + GLA_GATE_RANK], LANES)
    gate_lr = matmul_wt(h, w_lr_t, lambda j: j * LANES, LANES, 1024, LANES, F32)
    w2_pad = _pad_rows(gla_w_gate2[0], LANES)
    b_gate = gla_b_gate[0].reshape(1, GLA_KW)
    g_out = gla_g_out[0].reshape(1, GLA_VW)
    o_p, gla_state_p = gla(proj, gate_lr, w2_pad, b_gate, g_out, None, 0, n_p, len_p, 0, GLA_CHUNK, 1)
    o_s, gla_state_s = gla(proj, gate_lr, w2_pad, b_gate, g_out, state_gla, 0, n_s, len_s, t_p, len_s, 2)
    zero_conv = jnp.zeros((1, n_p, CONV_WIDTH - 1, CONV_CH), F32)
    y_p, conv_state_p = short_conv(proj, zero_conv, conv_w, 0, 0, n_p, len_p, 0, 256, 1)
    y_s, conv_state_s = short_conv(proj, state_conv, conv_w, 0, 0, n_s, len_s, t_p, CONV_CH, 2)
    x = mix_out(o_p, o_s, y_p, y_s, ab_w_out, (0,), x_p, x_s, 512, 512)
    x, h = moe_layer(x, 0, g_ffn_norm[0], *moe, g_mix_norm[1], False, t_p)

    w_mla_t = jnp.swapaxes(mla_w_in[0], 0, 1)
    c = matmul_wt(h, w_mla_t, lambda j: j * tn, Q_LORA + KV_LORA, 512, tn, F32)
    w_pe_t = _pad_rows(w_mla_t[Q_LORA + KV_LORA:], LANES)
    kpe_raw = matmul_wt(h, w_pe_t, lambda j: j * LANES, LANES, 1024, LANES, F32)
    pos = jnp.concatenate([jnp.tile(jnp.arange(len_p), n_p), jnp.tile(past + jnp.arange(len_s), n_s)])
    cos, sin = _rope_tables(pos)
    cqn, lat, lat_b, pe, pe_b = mla_post(c, kpe_raw, cos, sin, mla_g_q[0].reshape(1, -1), mla_g_kv[0].reshape(1, -1))
    w_uq = mla_w_uq[0].reshape(Q_LORA, MLA_HEADS, NOPE + ROPE)
    w_uq_rope = jnp.pad(w_uq[:, :, NOPE:], ((0, 0), (0, 0), (0, LANES - ROPE)))
    w_uq_re = jnp.concatenate([w_uq[:, :, :NOPE].reshape(Q_LORA, -1), w_uq_rope.reshape(Q_LORA, -1)], axis=1)
    qs = q_proj(cqn, w_uq_re, cos, sin)

    kn = matmul_ws(lat_b[:t_p], mla_w_uk, (0,), 0, MLA_HEADS * NOPE, 1024, 1024, BF16)
    vv = matmul_ws(lat_b[:t_p], mla_w_uv, (0,), 0, MLA_HEADS * V_DIM, 1024, 1024, BF16)
    attn_p = flash_prompt(qs, kn, pe_b, vv, n_p, len_p)

    q_lat, q_pe = absorb_q(qs, mla_w_uk, 0, t_p, n_s, len_s)
    cache_pe_t = jnp.swapaxes(cache_k_rope, 2, 3)
    o_lat = decode_attention(page_table, q_lat, q_pe, cache_kv_latent, cache_pe_t, lat, pe, 0, t_p, len_s)
    attn_s = expand_v(o_lat, mla_w_uv, 0, len_s)
    x = matmul_ws(attn_p, mla_w_out, (0,), 0, d, 512, 512, F32, res=x, a2=attn_s)
    y_p, y_s = moe_layer(x, 1, g_ffn_norm[1], *moe, g_final_norm, True, t_p)

    return (y_p.reshape(n_p, len_p, d), y_s.reshape(n_s, len_s, d),
            gla_state_p, conv_state_p,
            lat[:t_p].reshape(1, n_p, len_p, KV_LORA), pe[:t_p, :ROPE].reshape(1, n_p, len_p, ROPE),
            gla_state_s, conv_state_s,
            lat[t_p:].reshape(1, n_s, len_s, KV_LORA), pe[t_p:, :ROPE].reshape(1, n_s, len_s, ROPE))
```

```python
import functools

import jax
import jax.numpy as jnp
from jax import lax
from jax.experimental import pallas as pl
from jax.experimental.pallas import tpu as pltpu

F32, BF16, I32 = jnp.float32, jnp.bfloat16, jnp.int32

D_MODEL = 4096
RMS_EPS = 1e-6
PAGE_SIZE = 128

GLA_HEADS = 4
GLA_VW = D_MODEL // 2
GLA_KW = GLA_VW // 2
GLA_DV = GLA_VW // GLA_HEADS
GLA_DK = GLA_KW // GLA_HEADS
GLA_GATE_RANK = 16
GLA_TAU = 16.0
GLA_CHUNK = 64
CONV_CH = D_MODEL // 2
CONV_WIDTH = 3

MLA_HEADS = D_MODEL // 128
Q_LORA = D_MODEL // 4
KV_LORA = 512
NOPE = 128
ROPE = 64
V_DIM = 128
MLA_SCALE = (NOPE + ROPE) ** -0.5
ROPE_THETA = 10000.0

N_GROUPS = 4
EXP_PER_GROUP = 8
N_EXPERTS = N_GROUPS * EXP_PER_GROUP
TOP_K = 2
D_EXPERT = D_MODEL // 4

LANES = 128
SUBLANES = 8
VMEM_LIMIT_BYTES = 56 * 1024 * 1024

COL_Q, COL_K, COL_V, COL_R = 0, GLA_KW, 2 * GLA_KW, 2 * GLA_KW + GLA_VW
COL_B = COL_R + GLA_VW
COL_C = COL_B + CONV_CH
COL_U = COL_C + CONV_CH
AB_MAIN_COLS = COL_U + CONV_CH
AB_GATE_COL0 = 2 * GLA_KW + GLA_VW

MOE_TM = 256
MOE_TF = 512
MOE_TN = 2048
GATHER_TB = 128
GATHER_UNROLL = 8
PAGES_PER_STEP = 32
DECODE_CHUNK = 512
DECODE_UPDATE_CHUNKS = 2


def _params(n_axes):
    return pltpu.CompilerParams(dimension_semantics=("arbitrary",) * n_axes,
                                vmem_limit_bytes=VMEM_LIMIT_BYTES)


def _dot(a, b):
    return jnp.dot(a, b, preferred_element_type=F32)


def _dot_nt(a, b):
    return lax.dot_general(a, b, (((1,), (1,)), ((), ())), preferred_element_type=F32)


def _dot_tn(a, b):
    return lax.dot_general(a, b, (((0,), (0,)), ((), ())), preferred_element_type=F32)


def _rms(x, g):
    return x * lax.rsqrt(jnp.mean(x * x, axis=-1, keepdims=True) + RMS_EPS) * g


def _split2(x):
    hi = x.astype(BF16)
    return hi, (x - hi.astype(F32)).astype(BF16)


def _split3(x):
    hi = x.astype(BF16)
    r1 = x - hi.astype(F32)
    mid = r1.astype(BF16)
    lo = (r1 - mid.astype(F32)).astype(BF16)
    return hi, mid, lo


def _rmsnorm_body(x1_ref, x2_ref, g_ref, o_ref, *, split_step):
    i = pl.program_id(0)

    @pl.when(i < split_step)
    def _first_source():
        o_ref[...] = _rms(x1_ref[...], g_ref[...]).astype(o_ref.dtype)

    @pl.when(i >= split_step)
    def _second_source():
        o_ref[...] = _rms(x2_ref[...], g_ref[...]).astype(o_ref.dtype)


def rmsnorm_stacked(x1, x2, g, out_dtype, tm=256):
    m1, d = x1.shape
    m2 = x2.shape[0]
    split_step = m1 // tm
    return pl.pallas_call(
        functools.partial(_rmsnorm_body, split_step=split_step), grid=((m1 + m2) // tm,),
        in_specs=[pl.BlockSpec((tm, d), lambda i: (jnp.minimum(i, split_step - 1), 0)),
                  pl.BlockSpec((tm, d), lambda i: (jnp.maximum(i - split_step, 0), 0)),
                  pl.BlockSpec((1, d), lambda i: (0, 0))],
        out_specs=pl.BlockSpec((tm, d), lambda i: (i, 0)),
        out_shape=jax.ShapeDtypeStruct((m1 + m2, d), out_dtype),
        compiler_params=_params(1), name="rmsnorm")(x1, x2, g.reshape(1, d))


def _mm_body(*refs, has_res, w_rows_are_outputs, split_tile):
    a_ref, w_ref = refs[0], refs[1]
    refs = refs[2:]
    a2_ref = None
    if split_tile is not None:
        a2_ref, refs = refs[0], refs[1:]
    if has_res:
        res_ref, o_ref, wb_ref = refs
    else:
        o_ref, wb_ref = refs

    @pl.when(pl.program_id(1) == 0)
    def _cast_weights():
        w = w_ref[...]
        wb_ref[...] = (w.T if w_rows_are_outputs else w).astype(BF16)

    def tile(lhs_ref):
        acc = _dot(lhs_ref[...], wb_ref[...])
        if has_res:
            acc = acc + res_ref[...]
        o_ref[...] = acc.astype(o_ref.dtype)

    if split_tile is None:
        tile(a_ref)
    else:
        pl.when(pl.program_id(1) < split_tile)(lambda: tile(a_ref))
        pl.when(pl.program_id(1) >= split_tile)(lambda: tile(a2_ref))


def matmul_ws(a, w, lead, col0, n_cols, tm, tn, out_dtype, res=None, a2=None):
    m, k = a.shape
    cb0 = col0 // tn
    wspec = pl.BlockSpec((None,) * len(lead) + (k, tn), lambda j, i: tuple(lead) + (0, cb0 + j))
    split_tile = None
    if a2 is None:
        in_specs = [pl.BlockSpec((tm, k), lambda j, i: (i, 0)), wspec]
        args = [a, w]
    else:
        split_tile = m // tm
        m = m + a2.shape[0]
        in_specs = [pl.BlockSpec((tm, k), lambda j, i: (jnp.minimum(i, split_tile - 1), 0)), wspec,
                    pl.BlockSpec((tm, k), lambda j, i: (jnp.maximum(i - split_tile, 0), 0))]
        args = [a, w, a2]
    if res is not None:
        in_specs.append(pl.BlockSpec((tm, tn), lambda j, i: (i, j)))
        args.append(res)
    return pl.pallas_call(
        functools.partial(_mm_body, has_res=res is not None, w_rows_are_outputs=False, split_tile=split_tile),
        grid=(n_cols // tn, m // tm), in_specs=in_specs,
        out_specs=pl.BlockSpec((tm, tn), lambda j, i: (i, j)),
        out_shape=jax.ShapeDtypeStruct((m, n_cols), out_dtype),
        scratch_shapes=[pltpu.VMEM((k, tn), BF16)],
        compiler_params=_params(2), name="matmul_ws")(*args)


def matmul_wt(a, wt, row_of_tile, n_cols, tm, tn, out_dtype):
    m, k = a.shape
    wspec = pl.BlockSpec((pl.Element(tn), pl.Element(k)),
                         lambda j, i: (pl.multiple_of(row_of_tile(j), SUBLANES), 0))
    return pl.pallas_call(
        functools.partial(_mm_body, has_res=False, w_rows_are_outputs=True, split_tile=None),
        grid=(n_cols // tn, m // tm),
        in_specs=[pl.BlockSpec((tm, k), lambda j, i: (i, 0)), wspec],
        out_specs=pl.BlockSpec((tm, tn), lambda j, i: (i, j)),
        out_shape=jax.ShapeDtypeStruct((m, n_cols), out_dtype),
        scratch_shapes=[pltpu.VMEM((k, tn), BF16)],
        compiler_params=_params(2), name="matmul_wt")(a, wt)


def _mix_out_body(o1_ref, o2_ref, y1_ref, y2_ref, w_ref, r1_ref, r2_ref, out_ref, wb_ref, *, split_tile):
    i = pl.program_id(1)

    @pl.when(i == 0)
    def _cast_weights():
        wb_ref[...] = w_ref[...].astype(BF16)

    half = o1_ref.shape[1]

    def tile(o_ref, y_ref, r_ref):
        acc = _dot(o_ref[...], wb_ref[0:half, :]) + _dot(y_ref[...], wb_ref[half:2 * half, :])
        out_ref[...] = acc + r_ref[...]

    pl.when(i < split_tile)(lambda: tile(o1_ref, y1_ref, r1_ref))
    pl.when(i >= split_tile)(lambda: tile(o2_ref, y2_ref, r2_ref))


def mix_out(o1, o2, y1, y2, w, lead, res1, res2, tm, tn):
    m1, half = o1.shape
    m = m1 + o2.shape[0]
    n = w.shape[-1]
    split_tile = m1 // tm

    def first(j, i):
        return jnp.minimum(i, split_tile - 1)

    def second(j, i):
        return jnp.maximum(i - split_tile, 0)

    return pl.pallas_call(
        functools.partial(_mix_out_body, split_tile=split_tile), grid=(n // tn, m // tm),
        in_specs=[pl.BlockSpec((tm, half), lambda j, i: (first(j, i), 0)),
                  pl.BlockSpec((tm, half), lambda j, i: (second(j, i), 0)),
                  pl.BlockSpec((tm, half), lambda j, i: (first(j, i), 0)),
                  pl.BlockSpec((tm, half), lambda j, i: (second(j, i), 0)),
                  pl.BlockSpec((None,) * len(lead) + (2 * half, tn), lambda j, i: tuple(lead) + (0, j)),
                  pl.BlockSpec((tm, tn), lambda j, i: (first(j, i), j)),
                  pl.BlockSpec((tm, tn), lambda j, i: (second(j, i), j))],
        out_specs=pl.BlockSpec((tm, tn), lambda j, i: (i, j)),
        out_shape=jax.ShapeDtypeStruct((m, n), F32),
        scratch_shapes=[pltpu.VMEM((2 * half, tn), BF16)],
        compiler_params=_params(2), name="mix_out")(o1, o2, y1, y2, w, res1, res2)


def _log_sigmoid(x):
    return jnp.minimum(x, 0.0) - jnp.log1p(jnp.exp(-jnp.abs(x)))


def _gla_body(*refs, chunk, n_sub, has_s0):
    q_ref, k_ref, v_ref, r_ref, g_ref, w2_ref, bg_ref, go_ref = refs[:8]
    if has_s0:
        s0_ref, o_ref, s_ref = refs[8:]
    else:
        o_ref, s_ref = refs[8:]

    @pl.when(pl.program_id(1) == 0)
    def _init_state():
        s_ref[...] = s0_ref[...] if has_s0 else jnp.zeros_like(s_ref)

    gate_in = _dot(g_ref[...].astype(BF16), w2_ref[...].astype(BF16)) + bg_ref[...]
    log_a = _log_sigmoid(gate_in) / GLA_TAU

    row = lax.broadcasted_iota(I32, (chunk, chunk), 0)
    col = lax.broadcasted_iota(I32, (chunk, chunk), 1)
    causal = row >= col
    tri = jnp.where(causal, 1.0, 0.0).astype(BF16)
    ones = jnp.ones((chunk, GLA_DV), BF16)

    seq_outs = []
    for sq in range(n_sub):
        rs = slice(sq * chunk, (sq + 1) * chunk)
        parts = _split3(log_a[rs, :])
        b_all = _dot(tri, parts[0]) + _dot(tri, parts[1]) + _dot(tri, parts[2])
        head_outs = []
        for h in range(GLA_HEADS):
            ks = slice(h * GLA_DK, (h + 1) * GLA_DK)
            vs = slice(h * GLA_DV, (h + 1) * GLA_DV)
            b = b_all[:, ks]
            b_last = b[chunk - 1:chunk, :]
            b_tot_col = (_dot_tn(parts[0][:, ks], ones) + _dot_tn(parts[1][:, ks], ones)
                         + _dot_tn(parts[2][:, ks], ones))
            q = q_ref[rs, ks] * (GLA_DK ** -0.5)
            k = k_ref[rs, ks]
            v = v_ref[rs, vs].astype(BF16)
            q_dec = (q * jnp.exp(b)).astype(BF16)
            k_inv = (k * jnp.exp(-b)).astype(BF16)
            k_end = (k * jnp.exp(b_last - b)).astype(BF16)
            att = jnp.where(causal, _dot_nt(q_dec, k_inv), 0.0).astype(BF16)
            s_old = s_ref[sq, h]
            o = _dot(q_dec, s_old.astype(BF16)) + _dot(att, v)
            s_ref[sq, h] = jnp.exp(b_tot_col) * s_old + _dot_tn(k_end, v)
            r = r_ref[rs, vs]
            head_outs.append(_rms(o, go_ref[:, vs]) * (r * jax.nn.sigmoid(r)))
        seq_outs.append(jnp.concatenate(head_outs, axis=1))
    o_ref[...] = jnp.concatenate(seq_outs, axis=0).astype(o_ref.dtype)


def gla(proj, gate_lr, w2_pad, b_gate, g_out, s0, li, n_seq, seq_len, row0, chunk, n_sub):
    nc = seq_len // chunk
    assert n_sub == 1 or nc == 1
    rows_per_step = n_sub * chunk
    rb0 = row0 // rows_per_step

    def rows(b, c):
        return rb0 + b * nc + c

    state_block = (None, n_sub, GLA_HEADS, GLA_DK, GLA_DV)
    in_specs = [
        pl.BlockSpec((rows_per_step, GLA_KW), lambda b, c: (rows(b, c), COL_Q // GLA_KW)),
        pl.BlockSpec((rows_per_step, GLA_KW), lambda b, c: (rows(b, c), COL_K // GLA_KW)),
        pl.BlockSpec((rows_per_step, GLA_VW), lambda b, c: (rows(b, c), COL_V // GLA_VW)),
        pl.BlockSpec((rows_per_step, GLA_VW), lambda b, c: (rows(b, c), COL_R // GLA_VW)),
        pl.BlockSpec((rows_per_step, LANES), lambda b, c: (rows(b, c), 0)),
        pl.BlockSpec((LANES, GLA_KW), lambda b, c: (0, 0)),
        pl.BlockSpec((1, GLA_KW), lambda b, c: (0, 0)),
        pl.BlockSpec((1, GLA_VW), lambda b, c: (0, 0)),
    ]
    args = [proj, proj, proj, proj, gate_lr, w2_pad, b_gate, g_out]
    if s0 is not None:
        in_specs.append(pl.BlockSpec(state_block, lambda b, c: (li, b, 0, 0, 0)))
        args.append(s0)
    return pl.pallas_call(
        functools.partial(_gla_body, chunk=chunk, n_sub=n_sub, has_s0=s0 is not None),
        grid=(n_seq // n_sub, nc), in_specs=in_specs,
        out_specs=[pl.BlockSpec((rows_per_step, GLA_VW), lambda b, c: (b * nc + c, 0)),
                   pl.BlockSpec(state_block, lambda b, c: (0, b, 0, 0, 0))],
        out_shape=[jax.ShapeDtypeStruct((n_seq * seq_len, GLA_VW), BF16),
                   jax.ShapeDtypeStruct((1, n_seq, GLA_HEADS, GLA_DK, GLA_DV), F32)],
        compiler_params=_params(2), name="gla")(*args)


def _conv_body(gb_ref, gc_ref, u_ref, buf_ref, w_ref, o_ref, st_ref, sc_ref, *, seq_len, n_sub):
    w = w_ref[...]
    outs = []
    for sq in range(n_sub):
        rs = slice(sq * seq_len, (sq + 1) * seq_len)
        cu = gc_ref[rs, :] * u_ref[rs, :]
        sc_ref[0:8, :] = jnp.zeros((8, cu.shape[1]), F32)
        sc_ref[6:8, :] = buf_ref[sq]
        sc_ref[8:8 + seq_len, :] = cu
        z = sc_ref[6:6 + seq_len, :] * w[0:1, :] + sc_ref[7:7 + seq_len, :] * w[1:2, :] + cu * w[2:3, :]
        outs.append(gb_ref[rs, :] * z)
        st_ref[sq] = sc_ref[6 + seq_len:8 + seq_len, :]
    o_ref[...] = jnp.concatenate(outs, axis=0).astype(o_ref.dtype)


def short_conv(proj, buf, conv_w, li_buf, li_w, n_seq, seq_len, row0, ct, n_sub):
    rows_per_step = n_sub * seq_len
    rb0 = row0 // rows_per_step
    state_block = (None, n_sub, CONV_WIDTH - 1, ct)
    return pl.pallas_call(
        functools.partial(_conv_body, seq_len=seq_len, n_sub=n_sub),
        grid=(n_seq // n_sub, CONV_CH // ct),
        in_specs=[
            pl.BlockSpec((rows_per_step, ct), lambda b, j: (rb0 + b, COL_B // ct + j)),
            pl.BlockSpec((rows_per_step, ct), lambda b, j: (rb0 + b, COL_C // ct + j)),
            pl.BlockSpec((rows_per_step, ct), lambda b, j: (rb0 + b, COL_U // ct + j)),
            pl.BlockSpec(state_block, lambda b, j: (li_buf, b, 0, j)),
            pl.BlockSpec((None, CONV_WIDTH, ct), lambda b, j: (li_w, 0, j)),
        ],
        out_specs=[pl.BlockSpec((rows_per_step, ct), lambda b, j: (b, j)),
                   pl.BlockSpec(state_block, lambda b, j: (0, b, 0, j))],
        out_shape=[jax.ShapeDtypeStruct((n_seq * seq_len, CONV_CH), BF16),
                   jax.ShapeDtypeStruct((1, n_seq, CONV_WIDTH - 1, CONV_CH), F32)],
        scratch_shapes=[pltpu.VMEM((seq_len + 8, ct), F32)],
        compiler_params=_params(2), name="short_conv")(proj, proj, proj, buf, conv_w)


def _swap_halves(x, lane):
    return jnp.where(lane % ROPE < ROPE // 2, pltpu.roll(x, LANES - ROPE // 2, 1), pltpu.roll(x, ROPE // 2, 1))


def _mla_post_body(cq_ref, ckv_ref, kpe_ref, cos_ref, sin_ref, gq_ref, gkv_ref,
                   cqn_ref, lat_ref, latb_ref, pe_ref, peb_ref):
    cqn_ref[...] = _rms(cq_ref[...], gq_ref[...]).astype(BF16)
    lat = _rms(ckv_ref[...], gkv_ref[...])
    lat_ref[...] = lat
    latb_ref[...] = lat.astype(BF16)
    kpe = kpe_ref[...]
    lane = lax.broadcasted_iota(I32, kpe.shape, 1)
    pe = kpe * cos_ref[...] + _swap_halves(kpe, lane) * sin_ref[...]
    pe_ref[...] = pe
    peb_ref[...] = pe.astype(BF16)


def mla_post(c, kpe_raw, cos, sin, g_q, g_kv, tm=256):
    t = c.shape[0]
    return pl.pallas_call(
        _mla_post_body, grid=(t // tm,),
        in_specs=[
            pl.BlockSpec((tm, Q_LORA), lambda i: (i, 0)),
            pl.BlockSpec((tm, KV_LORA), lambda i: (i, Q_LORA // KV_LORA)),
            pl.BlockSpec((tm, LANES), lambda i: (i, 0)),
            pl.BlockSpec((tm, LANES), lambda i: (i, 0)),
            pl.BlockSpec((tm, LANES), lambda i: (i, 0)),
            pl.BlockSpec((1, Q_LORA), lambda i: (0, 0)),
            pl.BlockSpec((1, KV_LORA), lambda i: (0, 0)),
        ],
        out_specs=[
            pl.BlockSpec((tm, Q_LORA), lambda i: (i, 0)),
            pl.BlockSpec((tm, KV_LORA), lambda i: (i, 0)),
            pl.BlockSpec((tm, KV_LORA), lambda i: (i, 0)),
            pl.BlockSpec((tm, LANES), lambda i: (i, 0)),
            pl.BlockSpec((tm, LANES), lambda i: (i, 0)),
        ],
        out_shape=[
            jax.ShapeDtypeStruct((t, Q_LORA), BF16),
            jax.ShapeDtypeStruct((t, KV_LORA), F32),
            jax.ShapeDtypeStruct((t, KV_LORA), BF16),
            jax.ShapeDtypeStruct((t, LANES), F32),
            jax.ShapeDtypeStruct((t, LANES), BF16),
        ],
        compiler_params=_params(1), name="mla_post")(c, c, kpe_raw, cos, sin, g_q, g_kv)


def _q_proj_body(a_ref, w_ref, cos_ref, sin_ref, o_ref, wb_ref, *, n_nope_tiles):
    j = pl.program_id(0)

    @pl.when(pl.program_id(1) == 0)
    def _cast_weights():
        wb_ref[...] = w_ref[...].astype(BF16)

    acc = _dot(a_ref[...], wb_ref[...])

    @pl.when(j < n_nope_tiles)
    def _nope():
        o_ref[...] = (acc * MLA_SCALE).astype(BF16)

    @pl.when(j >= n_nope_tiles)
    def _rope():
        cos, sin = cos_ref[...], sin_ref[...]
        lane = lax.broadcasted_iota(I32, cos.shape, 1)
        for c in range(acc.shape[1] // LANES):
            x = acc[:, c * LANES:(c + 1) * LANES]
            y = x * cos + _swap_halves(x, lane) * sin
            o_ref[:, c * LANES:(c + 1) * LANES] = (y * MLA_SCALE).astype(BF16)


def q_proj(cqn, w_uq_re, cos, sin, tm=1024, tn=1024):
    m, k = cqn.shape
    n = w_uq_re.shape[1]
    return pl.pallas_call(
        functools.partial(_q_proj_body, n_nope_tiles=MLA_HEADS * NOPE // tn), grid=(n // tn, m // tm),
        in_specs=[pl.BlockSpec((tm, k), lambda j, i: (i, 0)),
                  pl.BlockSpec((k, tn), lambda j, i: (0, j)),
                  pl.BlockSpec((tm, LANES), lambda j, i: (i, 0)),
                  pl.BlockSpec((tm, LANES), lambda j, i: (i, 0))],
        out_specs=pl.BlockSpec((tm, tn), lambda j, i: (i, j)),
        out_shape=jax.ShapeDtypeStruct((m, n), BF16),
        scratch_shapes=[pltpu.VMEM((k, tn), BF16)],
        compiler_params=_params(2), name="q_proj")(cqn, w_uq_re, cos, sin)


def _flash_body(qn_ref, qp_ref, kn_ref, kp_ref, v_ref, o_ref, kcat_ref, *, tq):
    qi = pl.program_id(2)
    heads = [slice(h * NOPE, (h + 1) * NOPE) for h in range(2)]

    @pl.when(qi == 0)
    def _build_keys():
        for h, hs in enumerate(heads):
            kcat_ref[h, :, 0:NOPE] = kn_ref[:, hs]
            kcat_ref[h, :, NOPE:2 * NOPE] = kp_ref[...]

    q = [jnp.concatenate([qn_ref[:, hs], qp_ref[:, hs]], axis=1) for hs in heads]
    on_or_below_diag = (lax.broadcasted_iota(I32, (tq, tq), 0) >= lax.broadcasted_iota(I32, (tq, tq), 1))

    def block(j, carry, diagonal):
        ks = pl.ds(pl.multiple_of(j * tq, tq), tq)
        out = []
        for h, hs in enumerate(heads):
            m, l, acc = carry[h]
            s = _dot_nt(q[h], kcat_ref[h, ks, :])
            if diagonal:
                s = jnp.where(on_or_below_diag, s, -jnp.inf)
            m_new = jnp.maximum(m, jnp.max(s, axis=1, keepdims=True))
            alpha = jnp.exp(m - m_new)
            p = jnp.exp(s - m_new)
            l = alpha * l + jnp.sum(p, axis=1, keepdims=True)
            acc = alpha * acc + _dot(p.astype(BF16), v_ref[ks, hs])
            out.append((m_new, l, acc))
        return tuple(out)

    init = tuple((jnp.full((tq, 1), -jnp.inf, F32), jnp.zeros((tq, 1), F32), jnp.zeros((tq, V_DIM), F32))
                 for _ in heads)
    carry = lax.fori_loop(0, qi, lambda j, c: block(j, c, False), init)
    carry = block(qi, carry, True)
    for h, hs in enumerate(heads):
        _, l, acc = carry[h]
        o_ref[:, hs] = (acc / l).astype(o_ref.dtype)


def flash_prompt(qs, kn, kp, v, n_seq, seq_len, tq=512):
    nq = seq_len // tq
    hw = 2 * NOPE
    rope_cb0 = MLA_HEADS * NOPE // hw
    return pl.pallas_call(
        functools.partial(_flash_body, tq=tq),
        grid=(n_seq, MLA_HEADS // 2, nq),
        in_specs=[
            pl.BlockSpec((tq, hw), lambda b, h, i: (b * nq + i, h)),
            pl.BlockSpec((tq, hw), lambda b, h, i: (b * nq + i, rope_cb0 + h)),
            pl.BlockSpec((seq_len, hw), lambda b, h, i: (b, h)),
            pl.BlockSpec((seq_len, LANES), lambda b, h, i: (b, 0)),
            pl.BlockSpec((seq_len, hw), lambda b, h, i: (b, h)),
        ],
        out_specs=pl.BlockSpec((tq, hw), lambda b, h, i: (b * nq + i, h)),
        out_shape=jax.ShapeDtypeStruct((n_seq * seq_len, MLA_HEADS * V_DIM), BF16),
        scratch_shapes=[pltpu.VMEM((2, seq_len, 2 * NOPE), BF16)],
        compiler_params=_params(3), name="flash_prompt")(qs, qs, kn, kp, v)


def _absorb_q_body(qn_ref, qp_ref, w_ref, ql_ref, qpo_ref, *, n_seq, dec_seq):
    w = w_ref[...].astype(BF16)
    ql_ref[...] = _dot_nt(qn_ref[...], w).reshape(n_seq, dec_seq, KV_LORA)
    qpo_ref[...] = qp_ref[...].astype(F32).reshape(n_seq, dec_seq, LANES)


def absorb_q(qs, w_uk, li, row0, n_seq, dec_seq):
    n_rows = n_seq * dec_seq
    rope_cb0 = MLA_HEADS * NOPE // LANES
    return pl.pallas_call(
        functools.partial(_absorb_q_body, n_seq=n_seq, dec_seq=dec_seq), grid=(MLA_HEADS,),
        in_specs=[pl.BlockSpec((n_rows, NOPE), lambda h: (row0 // n_rows, h)),
                  pl.BlockSpec((n_rows, LANES), lambda h: (row0 // n_rows, rope_cb0 + h)),
                  pl.BlockSpec((None, KV_LORA, NOPE), lambda h: (li, 0, h))],
        out_specs=[pl.BlockSpec((n_seq, dec_seq, KV_LORA), lambda h: (0, h, 0)),
                   pl.BlockSpec((n_seq, dec_seq, LANES), lambda h: (0, h, 0))],
        out_shape=[jax.ShapeDtypeStruct((n_seq, MLA_HEADS * dec_seq, KV_LORA), F32),
                   jax.ShapeDtypeStruct((n_seq, MLA_HEADS * dec_seq, LANES), F32)],
        compiler_params=_params(1), name="absorb_q")(qs, qs, w_uk)


def _expand_v_body(a_ref, w_ref, o_ref):
    n_seq, dec_seq, _ = a_ref.shape
    a = a_ref[...].reshape(n_seq * dec_seq, KV_LORA).astype(BF16)
    o_ref[...] = _dot(a, w_ref[...].astype(BF16)).astype(o_ref.dtype)


def expand_v(o_lat, w_uv, li, dec_seq):
    n_seq = o_lat.shape[0]
    n_rows = n_seq * dec_seq
    return pl.pallas_call(
        _expand_v_body, grid=(MLA_HEADS,),
        in_specs=[pl.BlockSpec((n_seq, dec_seq, KV_LORA), lambda h: (0, h, 0)),
                  pl.BlockSpec((None, KV_LORA, V_DIM), lambda h: (li, 0, h))],
        out_specs=pl.BlockSpec((n_rows, V_DIM), lambda h: (0, h)),
        out_shape=jax.ShapeDtypeStruct((n_rows, MLA_HEADS * V_DIM), BF16),
        compiler_params=_params(1), name="expand_v")(o_lat, w_uv)


def _decode_body(pt_ref, ql_ref, qp_ref, latn_ref, pen_ref, lat_hbm, pe_hbm, o_ref,
                 lat_buf, pe_buf, sem, qlb_ref, qpb_ref, m_ref, l_ref, acc_ref, *, li, n_pages, dec_seq):
    s_idx = pl.program_id(1)
    n_steps = pl.num_programs(1)
    step = pl.program_id(0) * n_steps + s_idx
    n_rows = ql_ref.shape[0]

    def page_copies(seq, kv_step, slot):
        out = []
        for kk in range(n_pages):
            page = pt_ref[seq, kv_step * n_pages + kk]
            keys = pl.ds(kk * PAGE_SIZE, PAGE_SIZE)
            out.append(pltpu.make_async_copy(lat_hbm.at[li, page], lat_buf.at[slot, keys], sem.at[0, slot]))
            out.append(pltpu.make_async_copy(pe_hbm.at[li, page], pe_buf.at[slot, :, keys], sem.at[1, slot]))
        return out

    @pl.when(step == 0)
    def _first():
        for n, cp in enumerate(page_copies(0, 0, 0)):
            cp.start(priority=(n // 2) % 2)

    @pl.when(step + 1 < pl.num_programs(0) * n_steps)
    def _prefetch():
        nxt = step + 1
        for n, cp in enumerate(page_copies(nxt // n_steps, nxt % n_steps, nxt % 2)):
            cp.start(priority=(n // 2) % 2)

    @pl.when(s_idx == 0)
    def _init():
        qlb_ref[...] = ql_ref[...].astype(BF16)
        qpb_ref[...] = qp_ref[...].astype(BF16)
        m_ref[...] = jnp.full(m_ref.shape, -jnp.inf, F32)
        l_ref[...] = jnp.zeros(l_ref.shape, F32)
        acc_ref[...] = jnp.zeros(acc_ref.shape, F32)

    ql = qlb_ref[...]
    qp = qpb_ref[:, :ROPE]

    def update(scores, keys):
        m_prev = m_ref[...]
        m_new = m_prev
        for s in scores:
            m_new = jnp.maximum(m_new, jnp.max(s, axis=1, keepdims=True))
        alpha = jnp.exp(m_prev - m_new)
        l_new = alpha * l_ref[...]
        acc = alpha * acc_ref[...]
        for s, kc in zip(scores, keys):
            p = jnp.exp(s - m_new)
            l_new = l_new + jnp.sum(p, axis=1, keepdims=True)
            acc = acc + _dot(p.astype(BF16), kc)
        m_ref[...] = m_new
        l_ref[...] = l_new
        acc_ref[...] = acc

    slot = step % 2
    for cp in page_copies(pl.program_id(0), s_idx, slot):
        cp.wait()
    lat_slot = lat_buf.at[slot]
    pe_slot = pe_buf.at[slot]
    scores, keys = [], []
    for c in range(n_pages * PAGE_SIZE // DECODE_CHUNK):
        cs = slice(c * DECODE_CHUNK, (c + 1) * DECODE_CHUNK)
        kc = lat_slot[cs, :].astype(BF16)
        keys.append(kc)
        scores.append(_dot_nt(ql, kc) + _dot(qp, pe_slot[:, cs].astype(BF16)))
    for c0 in range(0, len(scores), DECODE_UPDATE_CHUNKS):
        update(scores[c0:c0 + DECODE_UPDATE_CHUNKS], keys[c0:c0 + DECODE_UPDATE_CHUNKS])

    @pl.when(s_idx == n_steps - 1)
    def _new_tokens():
        pad = PAGE_SIZE - dec_seq
        lat_new = jnp.concatenate([latn_ref[...], jnp.zeros((pad, KV_LORA), F32)], axis=0).astype(BF16)
        pe_new = jnp.concatenate([pen_ref[:, :ROPE], jnp.zeros((pad, ROPE), F32)], axis=0).astype(BF16)
        s = _dot_nt(ql, lat_new) + _dot_nt(qp, pe_new)
        q_tok = lax.broadcasted_iota(I32, (n_rows, PAGE_SIZE), 0) % dec_seq
        key_tok = lax.broadcasted_iota(I32, (n_rows, PAGE_SIZE), 1)
        update([jnp.where(key_tok <= q_tok, s, -jnp.inf)], [lat_new])
        o_ref[...] = acc_ref[...] / l_ref[...]


def decode_attention(page_table, q_lat, q_pe, cache_lat, cache_pe_t, lat_new, pe_new, li, row0, dec_seq):
    n_seq, n_rows, _ = q_lat.shape
    n_steps = page_table.shape[1] // PAGES_PER_STEP
    n_keys = PAGES_PER_STEP * PAGE_SIZE
    grid_spec = pltpu.PrefetchScalarGridSpec(
        num_scalar_prefetch=1, grid=(n_seq, n_steps),
        in_specs=[pl.BlockSpec((None, n_rows, KV_LORA), lambda b, s, pt: (b, 0, 0)),
                  pl.BlockSpec((None, n_rows, LANES), lambda b, s, pt: (b, 0, 0)),
                  pl.BlockSpec((dec_seq, KV_LORA), lambda b, s, pt: (row0 // dec_seq + b, 0)),
                  pl.BlockSpec((dec_seq, LANES), lambda b, s, pt: (row0 // dec_seq + b, 0)),
                  pl.BlockSpec(memory_space=pl.ANY), pl.BlockSpec(memory_space=pl.ANY)],
        out_specs=pl.BlockSpec((None, n_rows, KV_LORA), lambda b, s, pt: (b, 0, 0)),
        scratch_shapes=[pltpu.VMEM((2, n_keys, KV_LORA), F32), pltpu.VMEM((2, ROPE, n_keys), F32),
                        pltpu.SemaphoreType.DMA((2, 2)),
                        pltpu.VMEM((n_rows, KV_LORA), BF16), pltpu.VMEM((n_rows, LANES), BF16),
                        pltpu.VMEM((n_rows, 1), F32), pltpu.VMEM((n_rows, 1), F32),
                        pltpu.VMEM((n_rows, KV_LORA), F32)])
    return pl.pallas_call(
        functools.partial(_decode_body, li=li, n_pages=PAGES_PER_STEP, dec_seq=dec_seq),
        grid_spec=grid_spec, out_shape=jax.ShapeDtypeStruct((n_seq, n_rows, KV_LORA), F32),
        compiler_params=_params(2), name="decode_attention")(
            page_table, q_lat, q_pe, lat_new, pe_new, cache_lat, cache_pe_t)


def _router_body(x_ref, g_ref, w_ref, b_ref, o_ref):
    h = _rms(x_ref[...], g_ref[...])
    h_hi, h_lo = _split2(h)
    w_hi, w_lo = _split2(w_ref[...])
    logits = _dot(h_hi, w_hi) + _dot(h_hi, w_lo) + _dot(h_lo, w_hi) + b_ref[...]
    lane = lax.broadcasted_iota(I32, logits.shape, 1)
    lane_f = lane.astype(F32)
    neg = -jnp.inf
    lg = jnp.where(lane < N_GROUPS, logits, neg)
    mg = jnp.max(lg, axis=1, keepdims=True)
    grp = jnp.min(jnp.where(lg == mg, lane_f, float(LANES)), axis=1, keepdims=True)
    p_grp = 1.0 / jnp.sum(jnp.exp(lg - mg), axis=1, keepdims=True)
    e_lane = lane_f - float(N_GROUPS)
    in_grp = (lane >= N_GROUPS) & (lane < N_GROUPS + N_EXPERTS) & (jnp.floor(e_lane / EXP_PER_GROUP) == grp)
    le = jnp.where(in_grp, logits, neg)
    m1 = jnp.max(le, axis=1, keepdims=True)
    i1 = jnp.min(jnp.where(le == m1, lane_f, float(LANES)), axis=1, keepdims=True)
    le2 = jnp.where(lane_f == i1, neg, le)
    m2 = jnp.max(le2, axis=1, keepdims=True)
    i2 = jnp.min(jnp.where(le2 == m2, lane_f, float(LANES)), axis=1, keepdims=True)
    e2 = jnp.exp(m2 - m1)
    g1 = p_grp / (1.0 + e2)
    g2 = p_grp * e2 / (1.0 + e2)
    out = jnp.where(lane == 0, i1 - N_GROUPS, jnp.where(lane == 1, i2 - N_GROUPS,
                    jnp.where(lane == 2, g1, jnp.where(lane == 3, g2, 0.0))))
    o_ref[...] = out


def moe_router(x, g, w_router, b_router, tm=256):
    t, d = x.shape
    return pl.pallas_call(
        _router_body, grid=(t // tm,),
        in_specs=[pl.BlockSpec((tm, d), lambda i: (i, 0)), pl.BlockSpec((1, d), lambda i: (0, 0)),
                  pl.BlockSpec((d, LANES), lambda i: (0, 0)), pl.BlockSpec((1, LANES), lambda i: (0, 0))],
        out_specs=pl.BlockSpec((tm, LANES), lambda i: (i, 0)),
        out_shape=jax.ShapeDtypeStruct((t, LANES), F32),
        compiler_params=_params(1), name="moe_router")(x, g.reshape(1, d), w_router, b_router)


def _row_copy(src_hbm, dst, sem, src_row, dst_row):
    return pltpu.make_async_copy(src_hbm.at[pl.ds(src_row, 1)], dst.at[pl.ds(dst_row, 1)], sem)


def _rows_done(src_hbm, dst, sem):
    return pltpu.make_async_copy(src_hbm.at[pl.ds(0, dst.shape[0])], dst, sem)


def _dispatch_body(tok_ref, nsteps_ref, x_hbm, g_ref, o_ref, buf, sem, *, tb):
    i = pl.program_id(0)
    n_used = nsteps_ref[0]

    def start_all(step, slot):
        def body(g, c):
            for u in range(GATHER_UNROLL):
                r = g * GATHER_UNROLL + u
                _row_copy(x_hbm, buf.at[slot], sem.at[slot], tok_ref[step * tb + r], r).start(priority=u % 2)
            return c
        lax.fori_loop(0, tb // GATHER_UNROLL, body, 0)

    @pl.when(i == 0)
    def _first():
        start_all(0, 0)

    @pl.when(i + 1 < n_used)
    def _prefetch():
        start_all(i + 1, (i + 1) % 2)

    @pl.when(i < n_used)
    def _compute():
        slot = i % 2
        _rows_done(x_hbm, buf.at[slot], sem.at[slot]).wait()
        o_ref[...] = _rms(buf[slot], g_ref[...]).astype(o_ref.dtype)

    @pl.when(i >= n_used)
    def _unused_block():
        o_ref[...] = jnp.zeros_like(o_ref)


def moe_dispatch(tok_sorted, n_used_steps, x, g, tb=GATHER_TB):
    n_rows = tok_sorted.shape[0]
    d = x.shape[1]
    grid_spec = pltpu.PrefetchScalarGridSpec(
        num_scalar_prefetch=2, grid=(n_rows // tb,),
        in_specs=[pl.BlockSpec(memory_space=pl.ANY), pl.BlockSpec((1, d), lambda i, tok, ns: (0, 0))],
        out_specs=pl.BlockSpec((tb, d), lambda i, tok, ns: (i, 0)),
        scratch_shapes=[pltpu.VMEM((2, tb, d), F32), pltpu.SemaphoreType.DMA((2,))])
    return pl.pallas_call(
        functools.partial(_dispatch_body, tb=tb), grid_spec=grid_spec,
        out_shape=jax.ShapeDtypeStruct((n_rows, d), BF16),
        compiler_params=_params(1), name="moe_dispatch")(tok_sorted, n_used_steps, x, g.reshape(1, d))


def _weight_tile_stream(tabs, t, hbm_tiles, bufs, sem):
    e_ref, cw_ref, slot_ref, ne_ref, ncw_ref, has_next_ref = tabs

    def copies(e, cw, slot):
        return [pltpu.make_async_copy(src, buf.at[slot], sem.at[n, slot])
                for n, (src, buf) in enumerate(zip(hbm_tiles(e, cw), bufs))]

    slot = slot_ref[t]

    @pl.when(t == 0)
    def _first_group():
        for n, cp in enumerate(copies(e_ref[0], cw_ref[0], 0)):
            cp.start(priority=n % 2)

    @pl.when(has_next_ref[t] == 1)
    def _next_group():
        for n, cp in enumerate(copies(ne_ref[t], ncw_ref[t], 1 - slot)):
            cp.start(priority=n % 2)

    for cp in copies(e_ref[t], cw_ref[t], slot):
        cp.wait()
    return slot


def _moe_up_body(e_ref, cw_ref, rb_ref, co_ref, first_ref, valid_ref, slot_ref, ne_ref, ncw_ref, hn_ref,
                 x_ref, wg_hbm, wu_hbm, o_ref, wg_buf, wu_buf, sem, wgb, wub, *, layer):
    t = pl.program_id(0)

    @pl.when(first_ref[t] == 1)
    def _new_weights():
        def tiles(e, cw):
            cols = pl.ds(pl.multiple_of(cw * MOE_TF, MOE_TF), MOE_TF)
            return [wg_hbm.at[layer, e, :, cols], wu_hbm.at[layer, e, :, cols]]
        slot = _weight_tile_stream((e_ref, cw_ref, slot_ref, ne_ref, ncw_ref, hn_ref), t, tiles,
                                   (wg_buf, wu_buf), sem)
        wgb[...] = wg_buf[slot].astype(BF16)
        wub[...] = wu_buf[slot].astype(BF16)

    @pl.when(valid_ref[t] == 1)
    def _compute():
        x = x_ref[...]
        a = _dot(x, wgb[...])
        u = _dot(x, wub[...])
        o_ref[...] = (a * jax.nn.sigmoid(a) * u).astype(o_ref.dtype)

    @pl.when(valid_ref[t] == 0)
    def _unused_block():
        o_ref[...] = jnp.zeros_like(o_ref)


def moe_up(items, xs, w_gate, w_up, layer):
    n_rows, d = xs.shape
    n_items = items[0].shape[0]
    n_tabs = len(items)
    grid_spec = pltpu.PrefetchScalarGridSpec(
        num_scalar_prefetch=n_tabs, grid=(n_items,),
        in_specs=[pl.BlockSpec((MOE_TM, d), lambda t, *tabs: (tabs[2][t], 0)),
                  pl.BlockSpec(memory_space=pl.ANY), pl.BlockSpec(memory_space=pl.ANY)],
        out_specs=pl.BlockSpec((MOE_TM, MOE_TF), lambda t, *tabs: (tabs[2][t], tabs[3][t])),
        scratch_shapes=[pltpu.VMEM((2, d, MOE_TF), F32), pltpu.VMEM((2, d, MOE_TF), F32),
                        pltpu.SemaphoreType.DMA((2, 2)),
                        pltpu.VMEM((d, MOE_TF), BF16), pltpu.VMEM((d, MOE_TF), BF16)])
    return pl.pallas_call(
        functools.partial(_moe_up_body, layer=layer), grid_spec=grid_spec,
        out_shape=jax.ShapeDtypeStruct((n_rows, D_EXPERT), BF16),
        compiler_params=_params(1), name="moe_up")(*items, xs, w_gate, w_up)


def _moe_down_body(e_ref, cw_ref, rb_ref, co_ref, first_ref, valid_ref, slot_ref, ne_ref, ncw_ref, hn_ref,
                   h_ref, wd_hbm, o_ref, wd_buf, sem, wdb, *, layer):
    t = pl.program_id(0)

    @pl.when(first_ref[t] == 1)
    def _new_weights():
        def tiles(e, cw):
            return [wd_hbm.at[layer, e, :, pl.ds(pl.multiple_of(cw * MOE_TN, MOE_TN), MOE_TN)]]
        slot = _weight_tile_stream((e_ref, cw_ref, slot_ref, ne_ref, ncw_ref, hn_ref), t, tiles, (wd_buf,), sem)
        wdb[...] = wd_buf[slot].astype(BF16)

    @pl.when(valid_ref[t] == 1)
    def _compute():
        o_ref[...] = _dot(h_ref[...], wdb[...]).astype(o_ref.dtype)

    @pl.when(valid_ref[t] == 0)
    def _unused_block():
        o_ref[...] = jnp.zeros_like(o_ref)


def moe_down(items, hmid, w_down, layer):
    n_rows, f = hmid.shape
    d = w_down.shape[-1]
    n_items = items[0].shape[0]
    grid_spec = pltpu.PrefetchScalarGridSpec(
        num_scalar_prefetch=len(items), grid=(n_items,),
        in_specs=[pl.BlockSpec((MOE_TM, f), lambda t, *tabs: (tabs[2][t], 0)),
                  pl.BlockSpec(memory_space=pl.ANY)],
        out_specs=pl.BlockSpec((MOE_TM, MOE_TN), lambda t, *tabs: (tabs[2][t], tabs[3][t])),
        scratch_shapes=[pltpu.VMEM((2, f, MOE_TN), F32), pltpu.SemaphoreType.DMA((1, 2)),
                        pltpu.VMEM((f, MOE_TN), BF16)])
    return pl.pallas_call(
        functools.partial(_moe_down_body, layer=layer), grid_spec=grid_spec,
        out_shape=jax.ShapeDtypeStruct((n_rows, d), F32),
        compiler_params=_params(1), name="moe_down")(*items, hmid, w_down)


def _combine_body(pos_ref, x_ref, gate_ref, gn_ref, yb_hbm, o1_ref, o2_ref, buf, sem, *, tb, final, split_step):
    i = pl.program_id(0)
    n = pl.num_programs(0)

    def start_all(step, slot):
        def body(g, c):
            for u in range(GATHER_UNROLL):
                r = g * GATHER_UNROLL + u
                for kk in range(TOP_K):
                    _row_copy(yb_hbm, buf.at[slot, kk], sem.at[slot], pos_ref[(step * tb + r) * TOP_K + kk],
                              r).start(priority=kk % 2)
            return c
        lax.fori_loop(0, tb // GATHER_UNROLL, body, 0)

    @pl.when(i == 0)
    def _first():
        start_all(0, 0)

    @pl.when(i + 1 < n)
    def _prefetch():
        start_all(i + 1, (i + 1) % 2)

    slot = i % 2
    for kk in range(TOP_K):
        _rows_done(yb_hbm, buf.at[slot, kk], sem.at[slot]).wait()
    gate = gate_ref[...]
    y = x_ref[...] + (gate[:, 0:1] * buf[slot, 0] + gate[:, 1:2] * buf[slot, 1])
    if final:
        yn = _rms(y, gn_ref[...])

        @pl.when(i < split_step)
        def _prompt_rows():
            o1_ref[...] = yn

        @pl.when(i >= split_step)
        def _sample_rows():
            o2_ref[...] = yn
    else:
        o1_ref[...] = y
        o2_ref[...] = _rms(y, gn_ref[...]).astype(o2_ref.dtype)


def moe_combine(pos, x, gates, yb, g_norm, final, n_prompt_rows, tb=GATHER_TB):
    t, d = x.shape
    split_step = n_prompt_rows // tb
    if final:
        out_specs = [pl.BlockSpec((tb, d), lambda i, p: (jnp.minimum(i, split_step - 1), 0)),
                     pl.BlockSpec((tb, d), lambda i, p: (jnp.maximum(i - split_step, 0), 0))]
        out_shape = [jax.ShapeDtypeStruct((n_prompt_rows, d), F32),
                     jax.ShapeDtypeStruct((t - n_prompt_rows, d), F32)]
    else:
        out_specs = [pl.BlockSpec((tb, d), lambda i, p: (i, 0)), pl.BlockSpec((tb, d), lambda i, p: (i, 0))]
        out_shape = [jax.ShapeDtypeStruct((t, d), F32), jax.ShapeDtypeStruct((t, d), BF16)]
    grid_spec = pltpu.PrefetchScalarGridSpec(
        num_scalar_prefetch=1, grid=(t // tb,),
        in_specs=[pl.BlockSpec((tb, d), lambda i, p: (i, 0)), pl.BlockSpec((tb, TOP_K), lambda i, p: (i, 0)),
                  pl.BlockSpec((1, d), lambda i, p: (0, 0)), pl.BlockSpec(memory_space=pl.ANY)],
        out_specs=out_specs,
        scratch_shapes=[pltpu.VMEM((2, TOP_K, tb, d), F32), pltpu.SemaphoreType.DMA((2,))])
    return pl.pallas_call(
        functools.partial(_combine_body, tb=tb, final=final, split_step=split_step), grid_spec=grid_spec,
        out_shape=out_shape, compiler_params=_params(1), name="moe_combine")(
            pos, x, gates, g_norm.reshape(1, d), yb)


def _work_items(n_col_tiles, blocks_per_expert, block_start, n_used, n_blocks):
    n_items = n_col_tiles * n_blocks
    idx = jnp.arange(n_items, dtype=I32)
    n_valid = n_col_tiles * n_used
    valid = idx < n_valid
    t = jnp.minimum(idx, n_valid - 1)
    item_end = n_col_tiles * jnp.cumsum(blocks_per_expert)
    e = jnp.minimum(jnp.sum((t[:, None] >= item_end[None, :]).astype(I32), axis=1), N_EXPERTS - 1)
    nb = jnp.maximum(blocks_per_expert[e], 1)
    local = t - n_col_tiles * block_start[e]
    cw = (local // nb).astype(I32)
    r = local % nb
    spare = idx - n_valid
    rb = jnp.where(valid, block_start[e] + r, n_used + spare // n_col_tiles).astype(I32)
    co = jnp.where(valid, cw, spare % n_col_tiles).astype(I32)
    first = (r == 0) & valid
    slot = ((jnp.cumsum(first.astype(I32)) - 1) % 2).astype(I32)
    nxt = idx + nb
    has_next = first & (nxt < n_valid)
    nxt = jnp.minimum(nxt, n_items - 1)
    return (e, cw, rb, co, first.astype(I32), valid.astype(I32), slot, e[nxt], cw[nxt], has_next.astype(I32))


def moe_layer(x, layer, g_norm, w_group, b_group, w_expert, b_expert, w_gate, w_up, w_down,
              g_next, final, n_prompt_rows):
    t, d = x.shape
    pad = LANES - N_GROUPS - N_EXPERTS
    w_router = jnp.concatenate([w_group[layer], w_expert[layer], jnp.zeros((d, pad), F32)], axis=1)
    b_router = jnp.concatenate([b_group[layer], b_expert[layer], jnp.zeros((pad,), F32)]).reshape(1, LANES)
    routed = moe_router(x, g_norm, w_router, b_router)
    eid = routed[:, 0:TOP_K].astype(I32).reshape(-1)
    gates = routed[:, TOP_K:2 * TOP_K]

    n_assign = t * TOP_K
    onehot = (eid[:, None] == jnp.arange(N_EXPERTS, dtype=I32)[None, :]).astype(I32)
    rank = jnp.sum((jnp.cumsum(onehot, axis=0) - onehot) * onehot, axis=1)
    counts = jnp.sum(onehot, axis=0)
    blocks_per_expert = (counts + MOE_TM - 1) // MOE_TM
    block_start = jnp.cumsum(blocks_per_expert) - blocks_per_expert
    n_used = jnp.sum(blocks_per_expert)
    n_blocks = -(-(n_assign + N_EXPERTS * (MOE_TM - 1)) // MOE_TM)
    pos = (block_start[eid] * MOE_TM + rank).astype(I32)
    tok = jnp.arange(n_assign, dtype=I32) // TOP_K
    tok_sorted = jnp.zeros((n_blocks * MOE_TM,), I32).at[pos].set(tok)

    n_used_steps = (n_used * (MOE_TM // GATHER_TB)).astype(I32).reshape(1)
    xs = moe_dispatch(tok_sorted, n_used_steps, x, g_norm)
    hmid = moe_up(_work_items(D_EXPERT // MOE_TF, blocks_per_expert, block_start, n_used, n_blocks),
                  xs, w_gate, w_up, layer)
    yb = moe_down(_work_items(d // MOE_TN, blocks_per_expert, block_start, n_used, n_blocks),
                  hmid, w_down, layer)
    return moe_combine(pos, x, gates, yb, g_next, final, n_prompt_rows)


def _rope_tables(pos):
    inv = ROPE_THETA ** (-jnp.arange(0, ROPE, 2, dtype=F32) / ROPE)
    ang = pos.astype(F32)[:, None] * inv[None, :]
    cos, sin = jnp.cos(ang), jnp.sin(ang)
    return jnp.concatenate([cos] * 4, axis=1), jnp.concatenate([-sin, sin, -sin, sin], axis=1)


def _pad_rows(w, n_rows):
    return jnp.pad(w, ((0, n_rows - w.shape[0]), (0, 0)))


def kernel(x_prompt, x_sample, state_gla, state_conv, cache_kv_latent, cache_k_rope, page_table, g_mix_norm, g_ffn_norm, g_final_norm, ab_w_in, gla_w_gate2, gla_b_gate, gla_g_out, conv_w, ab_w_out, mla_w_in, mla_g_q, mla_g_kv, mla_w_uq, mla_w_uk, mla_w_uv, mla_w_out, moe_w_group, moe_b_group, moe_w_expert, moe_b_expert, moe_w_gate, moe_w_up, moe_w_down):
    n_p, len_p, d = x_prompt.shape
    n_s, len_s, _ = x_sample.shape
    t_p, t_s = n_p * len_p, n_s * len_s
    past = page_table.shape[1] * PAGE_SIZE
    x_p, x_s = x_prompt.reshape(t_p, d), x_sample.reshape(t_s, d)
    moe = (moe_w_group, moe_b_group, moe_w_expert, moe_b_expert, moe_w_gate, moe_w_up, moe_w_down)

    h = rmsnorm_stacked(x_p, x_s, g_mix_norm[0], BF16)
    w_in_t = jnp.swapaxes(ab_w_in[0], 0, 1)
    tn = 512
    proj = matmul_wt(h, w_in_t, lambda j: jnp.where(j * tn < AB_GATE_COL0, j * tn, j * tn + GLA_GATE_RANK),
                     AB_MAIN_COLS, 1024, tn, F32)
    w_lr_t = _pad_rows(w_in_t[AB_GATE_COL0:AB_GATE_COL0 + GLA_GATE_RANK], LANES)
    gate_lr = matmul_wt(h, w_lr_t, lambda j: j * LANES, LANES, 1024, LANES, F32)
    w2_pad = _pad_rows(gla_w_gate2[0], LANES)
    b_gate = gla_b_gate[0].reshape(1, GLA_KW)
    g_out = gla_g_out[0].reshape(1, GLA_VW)
    o_p, gla_state_p = gla(proj, gate_lr, w2_pad, b_gate, g_out, None, 0, n_p, len_p, 0, GLA_CHUNK, 1)
    o_s, gla_state_s = gla(proj, gate_lr, w2_pad, b_gate, g_out, state_gla, 0, n_s, len_s, t_p, len_s, 2)
    zero_conv = jnp.zeros((1, n_p, CONV_WIDTH - 1, CONV_CH), F32)
    y_p, conv_state_p = short_conv(proj, zero_conv, conv_w, 0, 0, n_p, len_p, 0, 256, 1)
    y_s, conv_state_s = short_conv(proj, state_conv, conv_w, 0, 0, n_s, len_s, t_p, CONV_CH, 2)
    x = mix_out(o_p, o_s, y_p, y_s, ab_w_out, (0,), x_p, x_s, 512, 512)
    x, h = moe_layer(x, 0, g_ffn_norm[0], *moe, g_mix_norm[1], False, t_p)

    w_mla_t = jnp.swapaxes(mla_w_in[0], 0, 1)
    c = matmul_wt(h, w_mla_t, lambda j: j * tn, Q_LORA + KV_LORA, 512, tn, F32)
    w_pe_t = _pad_rows(w_mla_t[Q_LORA + KV_LORA:], LANES)
    kpe_raw = matmul_wt(h, w_pe_t, lambda j: j * LANES, LANES, 1024, LANES, F32)
    pos = jnp.concatenate([jnp.tile(jnp.arange(len_p), n_p), jnp.tile(past + jnp.arange(len_s), n_s)])
    cos, sin = _rope_tables(pos)
    cqn, lat, lat_b, pe, pe_b = mla_post(c, kpe_raw, cos, sin, mla_g_q[0].reshape(1, -1), mla_g_kv[0].reshape(1, -1))
    w_uq = mla_w_uq[0].reshape(Q_LORA, MLA_HEADS, NOPE + ROPE)
    w_uq_rope = jnp.pad(w_uq[:, :, NOPE:], ((0, 0), (0, 0), (0, LANES - ROPE)))
    w_uq_re = jnp.concatenate([w_uq[:, :, :NOPE].reshape(Q_LORA, -1), w_uq_rope.reshape(Q_LORA, -1)], axis=1)
    qs = q_proj(cqn, w_uq_re, cos, sin)

    kn = matmul_ws(lat_b[:t_p], mla_w_uk, (0,), 0, MLA_HEADS * NOPE, 1024, 1024, BF16)
    vv = matmul_ws(lat_b[:t_p], mla_w_uv, (0,), 0, MLA_HEADS * V_DIM, 1024, 1024, BF16)
    attn_p = flash_prompt(qs, kn, pe_b, vv, n_p, len_p)

    q_lat, q_pe = absorb_q(qs, mla_w_uk, 0, t_p, n_s, len_s)
    cache_pe_t = jnp.swapaxes(cache_k_rope, 2, 3)
    o_lat = decode_attention(page_table, q_lat, q_pe, cache_kv_latent, cache_pe_t, lat, pe, 0, t_p, len_s)
    attn_s = expand_v(o_lat, mla_w_uv, 0, len_s)
    x = matmul_ws(attn_p, mla_w_out, (0,), 0, d, 512, 512, F32, res=x, a2=attn_s)
    y_p, y_s = moe_layer(x, 1, g_ffn_norm[1], *moe, g_final_norm, True, t_p)

    return (y_p.reshape(n_p, len_p, d), y_s.reshape(n_s, len_s, d),
            gla_state_p, conv_state_p,
            lat[:t_p].reshape(1, n_p, len_p, KV_LORA), pe[:t_p, :ROPE].reshape(1, n_p, len_p, ROPE),
            gla_state_s, conv_state_s,
            lat[t_p:].reshape(1, n_s, len_s, KV_LORA), pe[t_p:, :ROPE].reshape(1, n_s, len_s, ROPE))
```

```python
import functools

import jax
import jax.numpy as jnp
from jax import lax
from jax.experimental import pallas as pl
from jax.experimental.pallas import tpu as pltpu

F32, BF16, I32 = jnp.float32, jnp.bfloat16, jnp.int32

D_MODEL = 4096
RMS_EPS = 1e-6
PAGE_SIZE = 128

GLA_HEADS = 4
GLA_VW = D_MODEL // 2
GLA_KW = GLA_VW // 2
GLA_DV = GLA_VW // GLA_HEADS
GLA_DK = GLA_KW // GLA_HEADS
GLA_GATE_RANK = 16
GLA_TAU = 16.0
GLA_CHUNK = 64
CONV_CH = D_MODEL // 2
CONV_WIDTH = 3

MLA_HEADS = D_MODEL // 128
Q_LORA = D_MODEL // 4
KV_LORA = 512
NOPE = 128
ROPE = 64
V_DIM = 128
MLA_SCALE = (NOPE + ROPE) ** -0.5
ROPE_THETA = 10000.0

N_GROUPS = 4
EXP_PER_GROUP = 8
N_EXPERTS = N_GROUPS * EXP_PER_GROUP
TOP_K = 2
D_EXPERT = D_MODEL // 4

LANES = 128
SUBLANES = 8
VMEM_LIMIT_BYTES = 56 * 1024 * 1024

COL_Q, COL_K, COL_V, COL_R = 0, GLA_KW, 2 * GLA_KW, 2 * GLA_KW + GLA_VW
COL_B = COL_R + GLA_VW
COL_C = COL_B + CONV_CH
COL_U = COL_C + CONV_CH
AB_MAIN_COLS = COL_U + CONV_CH
AB_GATE_COL0 = 2 * GLA_KW + GLA_VW

MOE_TM = 256
MOE_TF = 512
MOE_TN = 2048
GATHER_TB = 128
GATHER_UNROLL = 8
PAGES_PER_STEP = 32
DECODE_CHUNK = 512
DECODE_UPDATE_CHUNKS = 2


def _params(n_axes):
    return pltpu.CompilerParams(dimension_semantics=("arbitrary",) * n_axes,
                                vmem_limit_bytes=VMEM_LIMIT_BYTES)


def _dot(a, b):
    return jnp.dot(a, b, preferred_element_type=F32)


def _dot_nt(a, b):
    return lax.dot_general(a, b, (((1,), (1,)), ((), ())), preferred_element_type=F32)


def _dot_tn(a, b):
    return lax.dot_general(a, b, (((0,), (0,)), ((), ())), preferred_element_type=F32)


def _rms(x, g):
    return x * lax.rsqrt(jnp.mean(x * x, axis=-1, keepdims=True) + RMS_EPS) * g


def _split2(x):
    hi = x.astype(BF16)
    return hi, (x - hi.astype(F32)).astype(BF16)


def _split3(x):
    hi = x.astype(BF16)
    r1 = x - hi.astype(F32)
    mid = r1.astype(BF16)
    lo = (r1 - mid.astype(F32)).astype(BF16)
    return hi, mid, lo


def _rmsnorm_body(x1_ref, x2_ref, g_ref, o_ref, *, split_step):
    i = pl.program_id(0)

    @pl.when(i < split_step)
    def _first_source():
        o_ref[...] = _rms(x1_ref[...], g_ref[...]).astype(o_ref.dtype)

    @pl.when(i >= split_step)
    def _second_source():
        o_ref[...] = _rms(x2_ref[...], g_ref[...]).astype(o_ref.dtype)


def rmsnorm_stacked(x1, x2, g, out_dtype, tm=256):
    m1, d = x1.shape
    m2 = x2.shape[0]
    split_step = m1 // tm
    return pl.pallas_call(
        functools.partial(_rmsnorm_body, split_step=split_step), grid=((m1 + m2) // tm,),
        in_specs=[pl.BlockSpec((tm, d), lambda i: (jnp.minimum(i, split_step - 1), 0)),
                  pl.BlockSpec((tm, d), lambda i: (jnp.maximum(i - split_step, 0), 0)),
                  pl.BlockSpec((1, d), lambda i: (0, 0))],
        out_specs=pl.BlockSpec((tm, d), lambda i: (i, 0)),
        out_shape=jax.ShapeDtypeStruct((m1 + m2, d), out_dtype),
        compiler_params=_params(1), name="rmsnorm")(x1, x2, g.reshape(1, d))


def _mm_body(*refs, has_res, w_rows_are_outputs, split_tile):
    a_ref, w_ref = refs[0], refs[1]
    refs = refs[2:]
    a2_ref = None
    if split_tile is not None:
        a2_ref, refs = refs[0], refs[1:]
    if has_res:
        res_ref, o_ref, wb_ref = refs
    else:
        o_ref, wb_ref = refs

    @pl.when(pl.program_id(1) == 0)
    def _cast_weights():
        w = w_ref[...]
        wb_ref[...] = (w.T if w_rows_are_outputs else w).astype(BF16)

    def tile(lhs_ref):
        acc = _dot(lhs_ref[...], wb_ref[...])
        if has_res:
            acc = acc + res_ref[...]
        o_ref[...] = acc.astype(o_ref.dtype)

    if split_tile is None:
        tile(a_ref)
    else:
        pl.when(pl.program_id(1) < split_tile)(lambda: tile(a_ref))
        pl.when(pl.program_id(1) >= split_tile)(lambda: tile(a2_ref))


def matmul_ws(a, w, lead, col0, n_cols, tm, tn, out_dtype, res=None, a2=None):
    m, k = a.shape
    cb0 = col0 // tn
    wspec = pl.BlockSpec((None,) * len(lead) + (k, tn), lambda j, i: tuple(lead) + (0, cb0 + j))
    split_tile = None
    if a2 is None:
        in_specs = [pl.BlockSpec((tm, k), lambda j, i: (i, 0)), wspec]
        args = [a, w]
    else:
        split_tile = m // tm
        m = m + a2.shape[0]
        in_specs = [pl.BlockSpec((tm, k), lambda j, i: (jnp.minimum(i, split_tile - 1), 0)), wspec,
                    pl.BlockSpec((tm, k), lambda j, i: (jnp.maximum(i - split_tile, 0), 0))]
        args = [a, w, a2]
    if res is not None:
        in_specs.append(pl.BlockSpec((tm, tn), lambda j, i: (i, j)))
        args.append(res)
    return pl.pallas_call(
        functools.partial(_mm_body, has_res=res is not None, w_rows_are_outputs=False, split_tile=split_tile),
        grid=(n_cols // tn, m // tm), in_specs=in_specs,
        out_specs=pl.BlockSpec((tm, tn), lambda j, i: (i, j)),
        out_shape=jax.ShapeDtypeStruct((m, n_cols), out_dtype),
        scratch_shapes=[pltpu.VMEM((k, tn), BF16)],
        compiler_params=_params(2), name="matmul_ws")(*args)


def matmul_wt(a, wt, row_of_tile, n_cols, tm, tn, out_dtype):
    m, k = a.shape
    wspec = pl.BlockSpec((pl.Element(tn), pl.Element(k)),
                         lambda j, i: (pl.multiple_of(row_of_tile(j), SUBLANES), 0))
    return pl.pallas_call(
        functools.partial(_mm_body, has_res=False, w_rows_are_outputs=True, split_tile=None),
        grid=(n_cols // tn, m // tm),
        in_specs=[pl.BlockSpec((tm, k), lambda j, i: (i, 0)), wspec],
        out_specs=pl.BlockSpec((tm, tn), lambda j, i: (i, j)),
        out_shape=jax.ShapeDtypeStruct((m, n_cols), out_dtype),
        scratch_shapes=[pltpu.VMEM((k, tn), BF16)],
        compiler_params=_params(2), name="matmul_wt")(a, wt)


def _mix_out_body(o1_ref, o2_ref, y1_ref, y2_ref, w_ref, r1_ref, r2_ref, out_ref, wb_ref, *, split_tile):
    i = pl.program_id(1)

    @pl.when(i == 0)
    def _cast_weights():
        wb_ref[...] = w_ref[...].astype(BF16)

    half = o1_ref.shape[1]

    def tile(o_ref, y_ref, r_ref):
        acc = _dot(o_ref[...], wb_ref[0:half, :]) + _dot(y_ref[...], wb_ref[half:2 * half, :])
        out_ref[...] = acc + r_ref[...]

    pl.when(i < split_tile)(lambda: tile(o1_ref, y1_ref, r1_ref))
    pl.when(i >= split_tile)(lambda: tile(o2_ref, y2_ref, r2_ref))


def mix_out(o1, o2, y1, y2, w, lead, res1, res2, tm, tn):
    m1, half = o1.shape
    m = m1 + o2.shape[0]
    n = w.shape[-1]
    split_tile = m1 // tm

    def first(j, i):
        return jnp.minimum(i, split_tile - 1)

    def second(j, i):
        return jnp.maximum(i - split_tile, 0)

    return pl.pallas_call(
        functools.partial(_mix_out_body, split_tile=split_tile), grid=(n // tn, m // tm),
        in_specs=[pl.BlockSpec((tm, half), lambda j, i: (first(j, i), 0)),
                  pl.BlockSpec((tm, half), lambda j, i: (second(j, i), 0)),
                  pl.BlockSpec((tm, half), lambda j, i: (first(j, i), 0)),
                  pl.BlockSpec((tm, half), lambda j, i: (second(j, i), 0)),
                  pl.BlockSpec((None,) * len(lead) + (2 * half, tn), lambda j, i: tuple(lead) + (0, j)),
                  pl.BlockSpec((tm, tn), lambda j, i: (first(j, i), j)),
                  pl.BlockSpec((tm, tn), lambda j, i: (second(j, i), j))],
        out_specs=pl.BlockSpec((tm, tn), lambda j, i: (i, j)),
        out_shape=jax.ShapeDtypeStruct((m, n), F32),
        scratch_shapes=[pltpu.VMEM((2 * half, tn), BF16)],
        compiler_params=_params(2), name="mix_out")(o1, o2, y1, y2, w, res1, res2)


def _log_sigmoid(x):
    return jnp.minimum(x, 0.0) - jnp.log1p(jnp.exp(-jnp.abs(x)))


def _gla_body(*refs, chunk, n_sub, has_s0):
    q_ref, k_ref, v_ref, r_ref, g_ref, w2_ref, bg_ref, go_ref = refs[:8]
    if has_s0:
        s0_ref, o_ref, s_ref = refs[8:]
    else:
        o_ref, s_ref = refs[8:]

    @pl.when(pl.program_id(1) == 0)
    def _init_state():
        s_ref[...] = s0_ref[...] if has_s0 else jnp.zeros_like(s_ref)

    gate_in = _dot(g_ref[...].astype(BF16), w2_ref[...].astype(BF16)) + bg_ref[...]
    log_a = _log_sigmoid(gate_in) / GLA_TAU

    row = lax.broadcasted_iota(I32, (chunk, chunk), 0)
    col = lax.broadcasted_iota(I32, (chunk, chunk), 1)
    causal = row >= col
    tri = jnp.where(causal, 1.0, 0.0).astype(BF16)
    ones = jnp.ones((chunk, GLA_DV), BF16)

    seq_outs = []
    for sq in range(n_sub):
        rs = slice(sq * chunk, (sq + 1) * chunk)
        parts = _split3(log_a[rs, :])
        b_all = _dot(tri, parts[0]) + _dot(tri, parts[1]) + _dot(tri, parts[2])
        head_outs = []
        for h in range(GLA_HEADS):
            ks = slice(h * GLA_DK, (h + 1) * GLA_DK)
            vs = slice(h * GLA_DV, (h + 1) * GLA_DV)
            b = b_all[:, ks]
            b_last = b[chunk - 1:chunk, :]
            b_tot_col = (_dot_tn(parts[0][:, ks], ones) + _dot_tn(parts[1][:, ks], ones)
                         + _dot_tn(parts[2][:, ks], ones))
            q = q_ref[rs, ks] * (GLA_DK ** -0.5)
            k = k_ref[rs, ks]
            v = v_ref[rs, vs].astype(BF16)
            q_dec = (q * jnp.exp(b)).astype(BF16)
            k_inv = (k * jnp.exp(-b)).astype(BF16)
            k_end = (k * jnp.exp(b_last - b)).astype(BF16)
            att = jnp.where(causal, _dot_nt(q_dec, k_inv), 0.0).astype(BF16)
            s_old = s_ref[sq, h]
            o = _dot(q_dec, s_old.astype(BF16)) + _dot(att, v)
            s_ref[sq, h] = jnp.exp(b_tot_col) * s_old + _dot_tn(k_end, v)
            r = r_ref[rs, vs]
            head_outs.append(_rms(o, go_ref[:, vs]) * (r * jax.nn.sigmoid(r)))
        seq_outs.append(jnp.concatenate(head_outs, axis=1))
    o_ref[...] = jnp.concatenate(seq_outs, axis=0).astype(o_ref.dtype)


def gla(proj, gate_lr, w2_pad, b_gate, g_out, s0, li, n_seq, seq_len, row0, chunk, n_sub):
    nc = seq_len // chunk
    assert n_sub == 1 or nc == 1
    rows_per_step = n_sub * chunk
    rb0 = row0 // rows_per_step

    def rows(b, c):
        return rb0 + b * nc + c

    state_block = (None, n_sub, GLA_HEADS, GLA_DK, GLA_DV)
    in_specs = [
        pl.BlockSpec((rows_per_step, GLA_KW), lambda b, c: (rows(b, c), COL_Q // GLA_KW)),
        pl.BlockSpec((rows_per_step, GLA_KW), lambda b, c: (rows(b, c), COL_K // GLA_KW)),
        pl.BlockSpec((rows_per_step, GLA_VW), lambda b, c: (rows(b, c), COL_V // GLA_VW)),
        pl.BlockSpec((rows_per_step, GLA_VW), lambda b, c: (rows(b, c), COL_R // GLA_VW)),
        pl.BlockSpec((rows_per_step, LANES), lambda b, c: (rows(b, c), 0)),
        pl.BlockSpec((LANES, GLA_KW), lambda b, c: (0, 0)),
        pl.BlockSpec((1, GLA_KW), lambda b, c: (0, 0)),
        pl.BlockSpec((1, GLA_VW), lambda b, c: (0, 0)),
    ]
    args = [proj, proj, proj, proj, gate_lr, w2_pad, b_gate, g_out]
    if s0 is not None:
        in_specs.append(pl.BlockSpec(state_block, lambda b, c: (li, b, 0, 0, 0)))
        args.append(s0)
    return pl.pallas_call(
        functools.partial(_gla_body, chunk=chunk, n_sub=n_sub, has_s0=s0 is not None),
        grid=(n_seq // n_sub, nc), in_specs=in_specs,
        out_specs=[pl.BlockSpec((rows_per_step, GLA_VW), lambda b, c: (b * nc + c, 0)),
                   pl.BlockSpec(state_block, lambda b, c: (0, b, 0, 0, 0))],
        out_shape=[jax.ShapeDtypeStruct((n_seq * seq_len, GLA_VW), BF16),
                   jax.ShapeDtypeStruct((1, n_seq, GLA_HEADS, GLA_DK, GLA_DV), F32)],
        compiler_params=_params(2), name="gla")(*args)


def _conv_body(gb_ref, gc_ref, u_ref, buf_ref, w_ref, o_ref, st_ref, sc_ref, *, seq_len, n_sub):
    w = w_ref[...]
    outs = []
    for sq in range(n_sub):
        rs = slice(sq * seq_len, (sq + 1) * seq_len)
        cu = gc_ref[rs, :] * u_ref[rs, :]
        sc_ref[0:8, :] = jnp.zeros((8, cu.shape[1]), F32)
        sc_ref[6:8, :] = buf_ref[sq]
        sc_ref[8:8 + seq_len, :] = cu
        z = sc_ref[6:6 + seq_len, :] * w[0:1, :] + sc_ref[7:7 + seq_len, :] * w[1:2, :] + cu * w[2:3, :]
        outs.append(gb_ref[rs, :] * z)
        st_ref[sq] = sc_ref[6 + seq_len:8 + seq_len, :]
    o_ref[...] = jnp.concatenate(outs, axis=0).astype(o_ref.dtype)


def short_conv(proj, buf, conv_w, li_buf, li_w, n_seq, seq_len, row0, ct, n_sub):
    rows_per_step = n_sub * seq_len
    rb0 = row0 // rows_per_step
    state_block = (None, n_sub, CONV_WIDTH - 1, ct)
    return pl.pallas_call(
        functools.partial(_conv_body, seq_len=seq_len, n_sub=n_sub),
        grid=(n_seq // n_sub, CONV_CH // ct),
        in_specs=[
            pl.BlockSpec((rows_per_step, ct), lambda b, j: (rb0 + b, COL_B // ct + j)),
            pl.BlockSpec((rows_per_step, ct), lambda b, j: (rb0 + b, COL_C // ct + j)),
            pl.BlockSpec((rows_per_step, ct), lambda b, j: (rb0 + b, COL_U // ct + j)),
            pl.BlockSpec(state_block, lambda b, j: (li_buf, b, 0, j)),
            pl.BlockSpec((None, CONV_WIDTH, ct), lambda b, j: (li_w, 0, j)),
        ],
        out_specs=[pl.BlockSpec((rows_per_step, ct), lambda b, j: (b, j)),
                   pl.BlockSpec(state_block, lambda b, j: (0, b, 0, j))],
        out_shape=[jax.ShapeDtypeStruct((n_seq * seq_len, CONV_CH), BF16),
                   jax.ShapeDtypeStruct((1, n_seq, CONV_WIDTH - 1, CONV_CH), F32)],
        scratch_shapes=[pltpu.VMEM((seq_len + 8, ct), F32)],
        compiler_params=_params(2), name="short_conv")(proj, proj, proj, buf, conv_w)


def _swap_halves(x, lane):
    return jnp.where(lane % ROPE < ROPE // 2, pltpu.roll(x, LANES - ROPE // 2, 1), pltpu.roll(x, ROPE // 2, 1))


def _mla_post_body(cq_ref, ckv_ref, kpe_ref, cos_ref, sin_ref, gq_ref, gkv_ref,
                   cqn_ref, lat_ref, latb_ref, pe_ref, peb_ref):
    cqn_ref[...] = _rms(cq_ref[...], gq_ref[...]).astype(BF16)
    lat = _rms(ckv_ref[...], gkv_ref[...])
    lat_ref[...] = lat
    latb_ref[...] = lat.astype(BF16)
    kpe = kpe_ref[...]
    lane = lax.broadcasted_iota(I32, kpe.shape, 1)
    pe = kpe * cos_ref[...] + _swap_halves(kpe, lane) * sin_ref[...]
    pe_ref[...] = pe
    peb_ref[...] = pe.astype(BF16)


def mla_post(c, kpe_raw, cos, sin, g_q, g_kv, tm=256):
    t = c.shape[0]
    return pl.pallas_call(
        _mla_post_body, grid=(t // tm,),
        in_specs=[
            pl.BlockSpec((tm, Q_LORA), lambda i: (i, 0)),
            pl.BlockSpec((tm, KV_LORA), lambda i: (i, Q_LORA // KV_LORA)),
            pl.BlockSpec((tm, LANES), lambda i: (i, 0)),
            pl.BlockSpec((tm, LANES), lambda i: (i, 0)),
            pl.BlockSpec((tm, LANES), lambda i: (i, 0)),
            pl.BlockSpec((1, Q_LORA), lambda i: (0, 0)),
            pl.BlockSpec((1, KV_LORA), lambda i: (0, 0)),
        ],
        out_specs=[
            pl.BlockSpec((tm, Q_LORA), lambda i: (i, 0)),
            pl.BlockSpec((tm, KV_LORA), lambda i: (i, 0)),
            pl.BlockSpec((tm, KV_LORA), lambda i: (i, 0)),
            pl.BlockSpec((tm, LANES), lambda i: (i, 0)),
            pl.BlockSpec((tm, LANES), lambda i: (i, 0)),
        ],
        out_shape=[
            jax.ShapeDtypeStruct((t, Q_LORA), BF16),
            jax.ShapeDtypeStruct((t, KV_LORA), F32),
            jax.ShapeDtypeStruct((t, KV_LORA), BF16),
            jax.ShapeDtypeStruct((t, LANES), F32),
            jax.ShapeDtypeStruct((t, LANES), BF16),
        ],
        compiler_params=_params(1), name="mla_post")(c, c, kpe_raw, cos, sin, g_q, g_kv)


def _q_proj_body(a_ref, w_ref, cos_ref, sin_ref, o_ref, wb_ref, *, n_nope_tiles):
    j = pl.program_id(0)

    @pl.when(pl.program_id(1) == 0)
    def _cast_weights():
        wb_ref[...] = w_ref[...].astype(BF16)

    acc = _dot(a_ref[...], wb_ref[...])

    @pl.when(j < n_nope_tiles)
    def _nope():
        o_ref[...] = (acc * MLA_SCALE).astype(BF16)

    @pl.when(j >= n_nope_tiles)
    def _rope():
        cos, sin = cos_ref[...], sin_ref[...]
        lane = lax.broadcasted_iota(I32, cos.shape, 1)
        for c in range(acc.shape[1] // LANES):
            x = acc[:, c * LANES:(c + 1) * LANES]
            y = x * cos + _swap_halves(x, lane) * sin
            o_ref[:, c * LANES:(c + 1) * LANES] = (y * MLA_SCALE).astype(BF16)


def q_proj(cqn, w_uq_re, cos, sin, tm=1024, tn=1024):
    m, k = cqn.shape
    n = w_uq_re.shape[1]
    return pl.pallas_call(
        functools.partial(_q_proj_body, n_nope_tiles=MLA_HEADS * NOPE // tn), grid=(n // tn, m // tm),
        in_specs=[pl.BlockSpec((tm, k), lambda j, i: (i, 0)),
                  pl.BlockSpec((k, tn), lambda j, i: (0, j)),
                  pl.BlockSpec((tm, LANES), lambda j, i: (i, 0)),
                  pl.BlockSpec((tm, LANES), lambda j, i: (i, 0))],
        out_specs=pl.BlockSpec((tm, tn), lambda j, i: (i, j)),
        out_shape=jax.ShapeDtypeStruct((m, n), BF16),
        scratch_shapes=[pltpu.VMEM((k, tn), BF16)],
        compiler_params=_params(2), name="q_proj")(cqn, w_uq_re, cos, sin)


def _flash_body(qn_ref, qp_ref, kn_ref, kp_ref, v_ref, o_ref, kcat_ref, *, tq):
    qi = pl.program_id(2)
    heads = [slice(h * NOPE, (h + 1) * NOPE) for h in range(2)]

    @pl.when(qi == 0)
    def _build_keys():
        for h, hs in enumerate(heads):
            kcat_ref[h, :, 0:NOPE] = kn_ref[:, hs]
            kcat_ref[h, :, NOPE:2 * NOPE] = kp_ref[...]

    q = [jnp.concatenate([qn_ref[:, hs], qp_ref[:, hs]], axis=1) for hs in heads]
    on_or_below_diag = (lax.broadcasted_iota(I32, (tq, tq), 0) >= lax.broadcasted_iota(I32, (tq, tq), 1))

    def block(j, carry, diagonal):
        ks = pl.ds(pl.multiple_of(j * tq, tq), tq)
        out = []
        for h, hs in enumerate(heads):
            m, l, acc = carry[h]
            s = _dot_nt(q[h], kcat_ref[h, ks, :])
            if diagonal:
                s = jnp.where(on_or_below_diag, s, -jnp.inf)
            m_new = jnp.maximum(m, jnp.max(s, axis=1, keepdims=True))
            alpha = jnp.exp(m - m_new)
            p = jnp.exp(s - m_new)
            l = alpha * l + jnp.sum(p, axis=1, keepdims=True)
            acc = alpha * acc + _dot(p.astype(BF16), v_ref[ks, hs])
            out.append((m_new, l, acc))
        return tuple(out)

    init = tuple((jnp.full((tq, 1), -jnp.inf, F32), jnp.zeros((tq, 1), F32), jnp.zeros((tq, V_DIM), F32))
                 for _ in heads)
    carry = lax.fori_loop(0, qi, lambda j, c: block(j, c, False), init)
    carry = block(qi, carry, True)
    for h, hs in enumerate(heads):
        _, l, acc = carry[h]
        o_ref[:, hs] = (acc / l).astype(o_ref.dtype)


def flash_prompt(qs, kn, kp, v, n_seq, seq_len, tq=512):
    nq = seq_len // tq
    hw = 2 * NOPE
    rope_cb0 = MLA_HEADS * NOPE // hw
    return pl.pallas_call(
        functools.partial(_flash_body, tq=tq),
        grid=(n_seq, MLA_HEADS // 2, nq),
        in_specs=[
            pl.BlockSpec((tq, hw), lambda b, h, i: (b * nq + i, h)),
            pl.BlockSpec((tq, hw), lambda b, h, i: (b * nq + i, rope_cb0 + h)),
            pl.BlockSpec((seq_len, hw), lambda b, h, i: (b, h)),
            pl.BlockSpec((seq_len, LANES), lambda b, h, i: (b, 0)),
            pl.BlockSpec((seq_len, hw), lambda b, h, i: (b, h)),
        ],
        out_specs=pl.BlockSpec((tq, hw), lambda b, h, i: (b * nq + i, h)),
        out_shape=jax.ShapeDtypeStruct((n_seq * seq_len, MLA_HEADS * V_DIM), BF16),
        scratch_shapes=[pltpu.VMEM((2, seq_len, 2 * NOPE), BF16)],
        compiler_params=_params(3), name="flash_prompt")(qs, qs, kn, kp, v)


def _absorb_q_body(qn_ref, qp_ref, w_ref, ql_ref, qpo_ref, *, n_seq, dec_seq):
    w = w_ref[...].astype(BF16)
    ql_ref[...] = _dot_nt(qn_ref[...], w).reshape(n_seq, dec_seq, KV_LORA)
    qpo_ref[...] = qp_ref[...].astype(F32).reshape(n_seq, dec_seq, LANES)


def absorb_q(qs, w_uk, li, row0, n_seq, dec_seq):
    n_rows = n_seq * dec_seq
    rope_cb0 = MLA_HEADS * NOPE // LANES
    return pl.pallas_call(
        functools.partial(_absorb_q_body, n_seq=n_seq, dec_seq=dec_seq), grid=(MLA_HEADS,),
        in_specs=[pl.BlockSpec((n_rows, NOPE), lambda h: (row0 // n_rows, h)),
                  pl.BlockSpec((n_rows, LANES), lambda h: (row0 // n_rows, rope_cb0 + h)),
                  pl.BlockSpec((None, KV_LORA, NOPE), lambda h: (li, 0, h))],
        out_specs=[pl.BlockSpec((n_seq, dec_seq, KV_LORA), lambda h: (0, h, 0)),
                   pl.BlockSpec((n_seq, dec_seq, LANES), lambda h: (0, h, 0))],
        out_shape=[jax.ShapeDtypeStruct((n_seq, MLA_HEADS * dec_seq, KV_LORA), F32),
                   jax.ShapeDtypeStruct((n_seq, MLA_HEADS * dec_seq, LANES), F32)],
        compiler_params=_params(1), name="absorb_q")(qs, qs, w_uk)


def _expand_v_body(a_ref, w_ref, o_ref):
    n_seq, dec_seq, _ = a_ref.shape
    a = a_ref[...].reshape(n_seq * dec_seq, KV_LORA).astype(BF16)
    o_ref[...] = _dot(a, w_ref[...].astype(BF16)).astype(o_ref.dtype)


def expand_v(o_lat, w_uv, li, dec_seq):
    n_seq = o_lat.shape[0]
    n_rows = n_seq * dec_seq
    return pl.pallas_call(
        _expand_v_body, grid=(MLA_HEADS,),
        in_specs=[pl.BlockSpec((n_seq, dec_seq, KV_LORA), lambda h: (0, h, 0)),
                  pl.BlockSpec((None, KV_LORA, V_DIM), lambda h: (li, 0, h))],
        out_specs=pl.BlockSpec((n_rows, V_DIM), lambda h: (0, h)),
        out_shape=jax.ShapeDtypeStruct((n_rows, MLA_HEADS * V_DIM), BF16),
        compiler_params=_params(1), name="expand_v")(o_lat, w_uv)


def _decode_body(pt_ref, ql_ref, qp_ref, latn_ref, pen_ref, lat_hbm, pe_hbm, o_ref,
                 lat_buf, pe_buf, sem, qlb_ref, qpb_ref, m_ref, l_ref, acc_ref, *, li, n_pages, dec_seq):
    s_idx = pl.program_id(1)
    n_steps = pl.num_programs(1)
    step = pl.program_id(0) * n_steps + s_idx
    n_rows = ql_ref.shape[0]

    def page_copies(seq, kv_step, slot):
        out = []
        for kk in range(n_pages):
            page = pt_ref[seq, kv_step * n_pages + kk]
            keys = pl.ds(kk * PAGE_SIZE, PAGE_SIZE)
            out.append(pltpu.make_async_copy(lat_hbm.at[li, page], lat_buf.at[slot, keys], sem.at[0, slot]))
            out.append(pltpu.make_async_copy(pe_hbm.at[li, page], pe_buf.at[slot, :, keys], sem.at[1, slot]))
        return out

    @pl.when(step == 0)
    def _first():
        for n, cp in enumerate(page_copies(0, 0, 0)):
            cp.start(priority=(n // 2) % 2)

    @pl.when(step + 1 < pl.num_programs(0) * n_steps)
    def _prefetch():
        nxt = step + 1
        for n, cp in enumerate(page_copies(nxt // n_steps, nxt % n_steps, nxt % 2)):
            cp.start(priority=(n // 2) % 2)

    @pl.when(s_idx == 0)
    def _init():
        qlb_ref[...] = ql_ref[...].astype(BF16)
        qpb_ref[...] = qp_ref[...].astype(BF16)
        m_ref[...] = jnp.full(m_ref.shape, -jnp.inf, F32)
        l_ref[...] = jnp.zeros(l_ref.shape, F32)
        acc_ref[...] = jnp.zeros(acc_ref.shape, F32)

    ql = qlb_ref[...]
    qp = qpb_ref[:, :ROPE]

    def update(scores, keys):
        m_prev = m_ref[...]
        m_new = m_prev
        for s in scores:
            m_new = jnp.maximum(m_new, jnp.max(s, axis=1, keepdims=True))
        alpha = jnp.exp(m_prev - m_new)
        l_new = alpha * l_ref[...]
        acc = alpha * acc_ref[...]
        for s, kc in zip(scores, keys):
            p = jnp.exp(s - m_new)
            l_new = l_new + jnp.sum(p, axis=1, keepdims=True)
            acc = acc + _dot(p.astype(BF16), kc)
        m_ref[...] = m_new
        l_ref[...] = l_new
        acc_ref[...] = acc

    slot = step % 2
    for cp in page_copies(pl.program_id(0), s_idx, slot):
        cp.wait()
    lat_slot = lat_buf.at[slot]
    pe_slot = pe_buf.at[slot]
    scores, keys = [], []
    for c in range(n_pages * PAGE_SIZE // DECODE_CHUNK):
        cs = slice(c * DECODE_CHUNK, (c + 1) * DECODE_CHUNK)
        kc = lat_slot[cs, :].astype(BF16)
        keys.append(kc)
        scores.append(_dot_nt(ql, kc) + _dot(qp, pe_slot[:, cs].astype(BF16)))
    for c0 in range(0, len(scores), DECODE_UPDATE_CHUNKS):
        update(scores[c0:c0 + DECODE_UPDATE_CHUNKS], keys[c0:c0 + DECODE_UPDATE_CHUNKS])

    @pl.when(s_idx == n_steps - 1)
    def _new_tokens():
        pad = PAGE_SIZE - dec_seq
        lat_new = jnp.concatenate([latn_ref[...], jnp.zeros((pad, KV_LORA), F32)], axis=0).astype(BF16)
        pe_new = jnp.concatenate([pen_ref[:, :ROPE], jnp.zeros((pad, ROPE), F32)], axis=0).astype(BF16)
        s = _dot_nt(ql, lat_new) + _dot_nt(qp, pe_new)
        q_tok = lax.broadcasted_iota(I32, (n_rows, PAGE_SIZE), 0) % dec_seq
        key_tok = lax.broadcasted_iota(I32, (n_rows, PAGE_SIZE), 1)
        update([jnp.where(key_tok <= q_tok, s, -jnp.inf)], [lat_new])
        o_ref[...] = acc_ref[...] / l_ref[...]


def decode_attention(page_table, q_lat, q_pe, cache_lat, cache_pe_t, lat_new, pe_new, li, row0, dec_seq):
    n_seq, n_rows, _ = q_lat.shape
    n_steps = page_table.shape[1] // PAGES_PER_STEP
    n_keys = PAGES_PER_STEP * PAGE_SIZE
    grid_spec = pltpu.PrefetchScalarGridSpec(
        num_scalar_prefetch=1, grid=(n_seq, n_steps),
        in_specs=[pl.BlockSpec((None, n_rows, KV_LORA), lambda b, s, pt: (b, 0, 0)),
                  pl.BlockSpec((None, n_rows, LANES), lambda b, s, pt: (b, 0, 0)),
                  pl.BlockSpec((dec_seq, KV_LORA), lambda b, s, pt: (row0 // dec_seq + b, 0)),
                  pl.BlockSpec((dec_seq, LANES), lambda b, s, pt: (row0 // dec_seq + b, 0)),
                  pl.BlockSpec(memory_space=pl.ANY), pl.BlockSpec(memory_space=pl.ANY)],
        out_specs=pl.BlockSpec((None, n_rows, KV_LORA), lambda b, s, pt: (b, 0, 0)),
        scratch_shapes=[pltpu.VMEM((2, n_keys, KV_LORA), F32), pltpu.VMEM((2, ROPE, n_keys), F32),
                        pltpu.SemaphoreType.DMA((2, 2)),
                        pltpu.VMEM((n_rows, KV_LORA), BF16), pltpu.VMEM((n_rows, LANES), BF16),
                        pltpu.VMEM((n_rows, 1), F32), pltpu.VMEM((n_rows, 1), F32),
                        pltpu.VMEM((n_rows, KV_LORA), F32)])
    return pl.pallas_call(
        functools.partial(_decode_body, li=li, n_pages=PAGES_PER_STEP, dec_seq=dec_seq),
        grid_spec=grid_spec, out_shape=jax.ShapeDtypeStruct((n_seq, n_rows, KV_LORA), F32),
        compiler_params=_params(2), name="decode_attention")(
            page_table, q_lat, q_pe, lat_new, pe_new, cache_lat, cache_pe_t)


def _router_body(x_ref, g_ref, w_ref, b_ref, o_ref):
    h = _rms(x_ref[...], g_ref[...])
    h_hi, h_lo = _split2(h)
    w_hi, w_lo = _split2(w_ref[...])
    logits = _dot(h_hi, w_hi) + _dot(h_hi, w_lo) + _dot(h_lo, w_hi) + b_ref[...]
    lane = lax.broadcasted_iota(I32, logits.shape, 1)
    lane_f = lane.astype(F32)
    neg = -jnp.inf
    lg = jnp.where(lane < N_GROUPS, logits, neg)
    mg = jnp.max(lg, axis=1, keepdims=True)
    grp = jnp.min(jnp.where(lg == mg, lane_f, float(LANES)), axis=1, keepdims=True)
    p_grp = 1.0 / jnp.sum(jnp.exp(lg - mg), axis=1, keepdims=True)
    e_lane = lane_f - float(N_GROUPS)
    in_grp = (lane >= N_GROUPS) & (lane < N_GROUPS + N_EXPERTS) & (jnp.floor(e_lane / EXP_PER_GROUP) == grp)
    le = jnp.where(in_grp, logits, neg)
    m1 = jnp.max(le, axis=1, keepdims=True)
    i1 = jnp.min(jnp.where(le == m1, lane_f, float(LANES)), axis=1, keepdims=True)
    le2 = jnp.where(lane_f == i1, neg, le)
    m2 = jnp.max(le2, axis=1, keepdims=True)
    i2 = jnp.min(jnp.where(le2 == m2, lane_f, float(LANES)), axis=1, keepdims=True)
    e2 = jnp.exp(m2 - m1)
    g1 = p_grp / (1.0 + e2)
    g2 = p_grp * e2 / (1.0 + e2)
    out = jnp.where(lane == 0, i1 - N_GROUPS, jnp.where(lane == 1, i2 - N_GROUPS,
                    jnp.where(lane == 2, g1, jnp.where(lane == 3, g2, 0.0))))
    o_ref[...] = out


def moe_router(x, g, w_router, b_router, tm=256):
    t, d = x.shape
    return pl.pallas_call(
        _router_body, grid=(t // tm,),
        in_specs=[pl.BlockSpec((tm, d), lambda i: (i, 0)), pl.BlockSpec((1, d), lambda i: (0, 0)),
                  pl.BlockSpec((d, LANES), lambda i: (0, 0)), pl.BlockSpec((1, LANES), lambda i: (0, 0))],
        out_specs=pl.BlockSpec((tm, LANES), lambda i: (i, 0)),
        out_shape=jax.ShapeDtypeStruct((t, LANES), F32),
        compiler_params=_params(1), name="moe_router")(x, g.reshape(1, d), w_router, b_router)


def _row_copy(src_hbm, dst, sem, src_row, dst_row):
    return pltpu.make_async_copy(src_hbm.at[pl.ds(src_row, 1)], dst.at[pl.ds(dst_row, 1)], sem)


def _rows_done(src_hbm, dst, sem):
    return pltpu.make_async_copy(src_hbm.at[pl.ds(0, dst.shape[0])], dst, sem)


def _dispatch_body(tok_ref, nsteps_ref, x_hbm, g_ref, o_ref, buf, sem, *, tb):
    i = pl.program_id(0)
    n_used = nsteps_ref[0]

    def start_all(step, slot):
        def body(g, c):
            for u in range(GATHER_UNROLL):
                r = g * GATHER_UNROLL + u
                _row_copy(x_hbm, buf.at[slot], sem.at[slot], tok_ref[step * tb + r], r).start(priority=u % 2)
            return c
        lax.fori_loop(0, tb // GATHER_UNROLL, body, 0)

    @pl.when(i == 0)
    def _first():
        start_all(0, 0)

    @pl.when(i + 1 < n_used)
    def _prefetch():
        start_all(i + 1, (i + 1) % 2)

    @pl.when(i < n_used)
    def _compute():
        slot = i % 2
        _rows_done(x_hbm, buf.at[slot], sem.at[slot]).wait()
        o_ref[...] = _rms(buf[slot], g_ref[...]).astype(o_ref.dtype)

    @pl.when(i >= n_used)
    def _unused_block():
        o_ref[...] = jnp.zeros_like(o_ref)


def moe_dispatch(tok_sorted, n_used_steps, x, g, tb=GATHER_TB):
    n_rows = tok_sorted.shape[0]
    d = x.shape[1]
    grid_spec = pltpu.PrefetchScalarGridSpec(
        num_scalar_prefetch=2, grid=(n_rows // tb,),
        in_specs=[pl.BlockSpec(memory_space=pl.ANY), pl.BlockSpec((1, d), lambda i, tok, ns: (0, 0))],
        out_specs=pl.BlockSpec((tb, d), lambda i, tok, ns: (i, 0)),
        scratch_shapes=[pltpu.VMEM((2, tb, d), F32), pltpu.SemaphoreType.DMA((2,))])
    return pl.pallas_call(
        functools.partial(_dispatch_body, tb=tb), grid_spec=grid_spec,
        out_shape=jax.ShapeDtypeStruct((n_rows, d), BF16),
        compiler_params=_params(1), name="moe_dispatch")(tok_sorted, n_used_steps, x, g.reshape(1, d))


def _weight_tile_stream(tabs, t, hbm_tiles, vmem_tiles, sem):
    e_ref, cw_ref, slot_ref, ne_ref, ncw_ref, has_next_ref = tabs

    def copies(e, cw, slot):
        return [pltpu.make_async_copy(src, dst, sem.at[n, slot])
                for n, (src, dst) in enumerate(zip(hbm_tiles(e, cw), vmem_tiles(slot)))]

    slot = slot_ref[t]

    @pl.when(t == 0)
    def _first_group():
        for n, cp in enumerate(copies(e_ref[0], cw_ref[0], 0)):
            cp.start(priority=n % 2)

    @pl.when(has_next_ref[t] == 1)
    def _next_group():
        for n, cp in enumerate(copies(ne_ref[t], ncw_ref[t], 1 - slot)):
            cp.start(priority=n % 2)

    for cp in copies(e_ref[t], cw_ref[t], slot):
        cp.wait()
    return slot


def _moe_up_body(e_ref, cw_ref, rb_ref, co_ref, first_ref, valid_ref, slot_ref, ne_ref, ncw_ref, hn_ref,
                 x_ref, wg_hbm, wu_hbm, o_ref, wg_buf, wu_buf, sem, wgb, wub, *, layer):
    t = pl.program_id(0)

    @pl.when(first_ref[t] == 1)
    def _new_weights():
        def tiles(e, cw):
            cols = pl.ds(pl.multiple_of(cw * MOE_TF, MOE_TF), MOE_TF)
            return [wg_hbm.at[layer, e, :, cols], wu_hbm.at[layer, e, :, cols]]
        slot = _weight_tile_stream((e_ref, cw_ref, slot_ref, ne_ref, ncw_ref, hn_ref), t, tiles,
                                   lambda s: [wg_buf.at[s], wu_buf.at[s]], sem)
        wgb[...] = wg_buf[slot].astype(BF16)
        wub[...] = wu_buf[slot].astype(BF16)

    @pl.when(valid_ref[t] == 1)
    def _compute():
        x = x_ref[...]
        a = _dot(x, wgb[...])
        u = _dot(x, wub[...])
        o_ref[...] = (a * jax.nn.sigmoid(a) * u).astype(o_ref.dtype)

    @pl.when(valid_ref[t] == 0)
    def _unused_block():
        o_ref[...] = jnp.zeros_like(o_ref)


def moe_up(items, xs, w_gate, w_up, layer):
    n_rows, d = xs.shape
    n_items = items[0].shape[0]
    n_tabs = len(items)
    grid_spec = pltpu.PrefetchScalarGridSpec(
        num_scalar_prefetch=n_tabs, grid=(n_items,),
        in_specs=[pl.BlockSpec((MOE_TM, d), lambda t, *tabs: (tabs[2][t], 0)),
                  pl.BlockSpec(memory_space=pl.ANY), pl.BlockSpec(memory_space=pl.ANY)],
        out_specs=pl.BlockSpec((MOE_TM, MOE_TF), lambda t, *tabs: (tabs[2][t], tabs[3][t])),
        scratch_shapes=[pltpu.VMEM((2, d, MOE_TF), F32), pltpu.VMEM((2, d, MOE_TF), F32),
                        pltpu.SemaphoreType.DMA((2, 2)),
                        pltpu.VMEM((d, MOE_TF), BF16), pltpu.VMEM((d, MOE_TF), BF16)])
    return pl.pallas_call(
        functools.partial(_moe_up_body, layer=layer), grid_spec=grid_spec,
        out_shape=jax.ShapeDtypeStruct((n_rows, D_EXPERT), BF16),
        compiler_params=_params(1), name="moe_up")(*items, xs, w_gate, w_up)


def _moe_down_body(e_ref, cw_ref, rb_ref, co_ref, first_ref, valid_ref, slot_ref, ne_ref, ncw_ref, hn_ref,
                   h_ref, wd_hbm, o_ref, wd_buf, sem, wdb, *, layer):
    t = pl.program_id(0)

    @pl.when(first_ref[t] == 1)
    def _new_weights():
        half = wd_buf.shape[1] // 2

        def tiles(e, cw):
            cols = pl.ds(pl.multiple_of(cw * MOE_TN, MOE_TN), MOE_TN)
            return [wd_hbm.at[layer, e, pl.ds(0, half), cols], wd_hbm.at[layer, e, pl.ds(half, half), cols]]
        slot = _weight_tile_stream((e_ref, cw_ref, slot_ref, ne_ref, ncw_ref, hn_ref), t, tiles,
                                   lambda s: [wd_buf.at[s, pl.ds(0, half)], wd_buf.at[s, pl.ds(half, half)]], sem)
        wdb[...] = wd_buf[slot].astype(BF16)

    @pl.when(valid_ref[t] == 1)
    def _compute():
        o_ref[...] = _dot(h_ref[...], wdb[...]).astype(o_ref.dtype)

    @pl.when(valid_ref[t] == 0)
    def _unused_block():
        o_ref[...] = jnp.zeros_like(o_ref)


def moe_down(items, hmid, w_down, layer):
    n_rows, f = hmid.shape
    d = w_down.shape[-1]
    n_items = items[0].shape[0]
    grid_spec = pltpu.PrefetchScalarGridSpec(
        num_scalar_prefetch=len(items), grid=(n_items,),
        in_specs=[pl.BlockSpec((MOE_TM, f), lambda t, *tabs: (tabs[2][t], 0)),
                  pl.BlockSpec(memory_space=pl.ANY)],
        out_specs=pl.BlockSpec((MOE_TM, MOE_TN), lambda t, *tabs: (tabs[2][t], tabs[3][t])),
        scratch_shapes=[pltpu.VMEM((2, f, MOE_TN), F32), pltpu.SemaphoreType.DMA((2, 2)),
                        pltpu.VMEM((f, MOE_TN), BF16)])
    return pl.pallas_call(
        functools.partial(_moe_down_body, layer=layer), grid_spec=grid_spec,
        out_shape=jax.ShapeDtypeStruct((n_rows, d), F32),
        compiler_params=_params(1), name="moe_down")(*items, hmid, w_down)


def _combine_body(pos_ref, x_ref, gate_ref, gn_ref, yb_hbm, o1_ref, o2_ref, buf, sem, *, tb, final, split_step):
    i = pl.program_id(0)
    n = pl.num_programs(0)

    def start_all(step, slot):
        def body(g, c):
            for u in range(GATHER_UNROLL):
                r = g * GATHER_UNROLL + u
                for kk in range(TOP_K):
                    _row_copy(yb_hbm, buf.at[slot, kk], sem.at[slot], pos_ref[(step * tb + r) * TOP_K + kk],
                              r).start(priority=kk % 2)
            return c
        lax.fori_loop(0, tb // GATHER_UNROLL, body, 0)

    @pl.when(i == 0)
    def _first():
        start_all(0, 0)

    @pl.when(i + 1 < n)
    def _prefetch():
        start_all(i + 1, (i + 1) % 2)

    slot = i % 2
    for kk in range(TOP_K):
        _rows_done(yb_hbm, buf.at[slot, kk], sem.at[slot]).wait()
    gate = gate_ref[...]
    y = x_ref[...] + (gate[:, 0:1] * buf[slot, 0] + gate[:, 1:2] * buf[slot, 1])
    if final:
        yn = _rms(y, gn_ref[...])

        @pl.when(i < split_step)
        def _prompt_rows():
            o1_ref[...] = yn

        @pl.when(i >= split_step)
        def _sample_rows():
            o2_ref[...] = yn
    else:
        o1_ref[...] = y
        o2_ref[...] = _rms(y, gn_ref[...]).astype(o2_ref.dtype)


def moe_combine(pos, x, gates, yb, g_norm, final, n_prompt_rows, tb=GATHER_TB):
    t, d = x.shape
    split_step = n_prompt_rows // tb
    if final:
        out_specs = [pl.BlockSpec((tb, d), lambda i, p: (jnp.minimum(i, split_step - 1), 0)),
                     pl.BlockSpec((tb, d), lambda i, p: (jnp.maximum(i - split_step, 0), 0))]
        out_shape = [jax.ShapeDtypeStruct((n_prompt_rows, d), F32),
                     jax.ShapeDtypeStruct((t - n_prompt_rows, d), F32)]
    else:
        out_specs = [pl.BlockSpec((tb, d), lambda i, p: (i, 0)), pl.BlockSpec((tb, d), lambda i, p: (i, 0))]
        out_shape = [jax.ShapeDtypeStruct((t, d), F32), jax.ShapeDtypeStruct((t, d), BF16)]
    grid_spec = pltpu.PrefetchScalarGridSpec(
        num_scalar_prefetch=1, grid=(t // tb,),
        in_specs=[pl.BlockSpec((tb, d), lambda i, p: (i, 0)), pl.BlockSpec((tb, TOP_K), lambda i, p: (i, 0)),
                  pl.BlockSpec((1, d), lambda i, p: (0, 0)), pl.BlockSpec(memory_space=pl.ANY)],
        out_specs=out_specs,
        scratch_shapes=[pltpu.VMEM((2, TOP_K, tb, d), F32), pltpu.SemaphoreType.DMA((2,))])
    return pl.pallas_call(
        functools.partial(_combine_body, tb=tb, final=final, split_step=split_step), grid_spec=grid_spec,
        out_shape=out_shape, compiler_params=_params(1), name="moe_combine")(
            pos, x, gates, g_norm.reshape(1, d), yb)


def _work_items(n_col_tiles, blocks_per_expert, block_start, n_used, n_blocks):
    n_items = n_col_tiles * n_blocks
    idx = jnp.arange(n_items, dtype=I32)
    n_valid = n_col_tiles * n_used
    valid = idx < n_valid
    t = jnp.minimum(idx, n_valid - 1)
    item_end = n_col_tiles * jnp.cumsum(blocks_per_expert)
    e = jnp.minimum(jnp.sum((t[:, None] >= item_end[None, :]).astype(I32), axis=1), N_EXPERTS - 1)
    nb = jnp.maximum(blocks_per_expert[e], 1)
    local = t - n_col_tiles * block_start[e]
    cw = (local // nb).astype(I32)
    r = local % nb
    spare = idx - n_valid
    rb = jnp.where(valid, block_start[e] + r, n_used + spare // n_col_tiles).astype(I32)
    co = jnp.where(valid, cw, spare % n_col_tiles).astype(I32)
    first = (r == 0) & valid
    slot = ((jnp.cumsum(first.astype(I32)) - 1) % 2).astype(I32)
    nxt = idx + nb
    has_next = first & (nxt < n_valid)
    nxt = jnp.minimum(nxt, n_items - 1)
    return (e, cw, rb, co, first.astype(I32), valid.astype(I32), slot, e[nxt], cw[nxt], has_next.astype(I32))


def moe_layer(x, layer, g_norm, w_group, b_group, w_expert, b_expert, w_gate, w_up, w_down,
              g_next, final, n_prompt_rows):
    t, d = x.shape
    pad = LANES - N_GROUPS - N_EXPERTS
    w_router = jnp.concatenate([w_group[layer], w_expert[layer], jnp.zeros((d, pad), F32)], axis=1)
    b_router = jnp.concatenate([b_group[layer], b_expert[layer], jnp.zeros((pad,), F32)]).reshape(1, LANES)
    routed = moe_router(x, g_norm, w_router, b_router)
    eid = routed[:, 0:TOP_K].astype(I32).reshape(-1)
    gates = routed[:, TOP_K:2 * TOP_K]

    n_assign = t * TOP_K
    onehot = (eid[:, None] == jnp.arange(N_EXPERTS, dtype=I32)[None, :]).astype(I32)
    rank = jnp.sum((jnp.cumsum(onehot, axis=0) - onehot) * onehot, axis=1)
    counts = jnp.sum(onehot, axis=0)
    blocks_per_expert = (counts + MOE_TM - 1) // MOE_TM
    block_start = jnp.cumsum(blocks_per_expert) - blocks_per_expert
    n_used = jnp.sum(blocks_per_expert)
    n_blocks = -(-(n_assign + N_EXPERTS * (MOE_TM - 1)) // MOE_TM)
    pos = (block_start[eid] * MOE_TM + rank).astype(I32)
    tok = jnp.arange(n_assign, dtype=I32) // TOP_K
    tok_sorted = jnp.zeros((n_blocks * MOE_TM,), I32).at[pos].set(tok)

    n_used_steps = (n_used * (MOE_TM // GATHER_TB)).astype(I32).reshape(1)
    xs = moe_dispatch(tok_sorted, n_used_steps, x, g_norm)
    hmid = moe_up(_work_items(D_EXPERT // MOE_TF, blocks_per_expert, block_start, n_used, n_blocks),
                  xs, w_gate, w_up, layer)
    yb = moe_down(_work_items(d // MOE_TN, blocks_per_expert, block_start, n_used, n_blocks),
                  hmid, w_down, layer)
    return moe_combine(pos, x, gates, yb, g_next, final, n_prompt_rows)


def _rope_tables(pos):
    inv = ROPE_THETA ** (-jnp.arange(0, ROPE, 2, dtype=F32) / ROPE)
    ang = pos.astype(F32)[:, None] * inv[None, :]
    cos, sin = jnp.cos(ang), jnp.sin(ang)
    return jnp.concatenate([cos] * 4, axis=1), jnp.concatenate([-sin, sin, -sin, sin], axis=1)


def _pad_rows(w, n_rows):
    return jnp.pad(w, ((0, n_rows - w.shape[0]), (0, 0)))


def kernel(x_prompt, x_sample, state_gla, state_conv, cache_kv_latent, cache_k_rope, page_table, g_mix_norm, g_ffn_norm, g_final_norm, ab_w_in, gla_w_gate2, gla_b_gate, gla_g_out, conv_w, ab_w_out, mla_w_in, mla_g_q, mla_g_kv, mla_w_uq, mla_w_uk, mla_w_uv, mla_w_out, moe_w_group, moe_b_group, moe_w_expert, moe_b_expert, moe_w_gate, moe_w_up, moe_w_down):
    n_p, len_p, d = x_prompt.shape
    n_s, len_s, _ = x_sample.shape
    t_p, t_s = n_p * len_p, n_s * len_s
    past = page_table.shape[1] * PAGE_SIZE
    x_p, x_s = x_prompt.reshape(t_p, d), x_sample.reshape(t_s, d)
    moe = (moe_w_group, moe_b_group, moe_w_expert, moe_b_expert, moe_w_gate, moe_w_up, moe_w_down)

    h = rmsnorm_stacked(x_p, x_s, g_mix_norm[0], BF16)
    w_in_t = jnp.swapaxes(ab_w_in[0], 0, 1)
    tn = 512
    proj = matmul_wt(h, w_in_t, lambda j: jnp.where(j * tn < AB_GATE_COL0, j * tn, j * tn + GLA_GATE_RANK),
                     AB_MAIN_COLS, 1024, tn, F32)
    w_lr_t = _pad_rows(w_in_t[AB_GATE_COL0:AB_GATE_COL0 + GLA_GATE_RANK], LANES)
    gate_lr = matmul_wt(h, w_lr_t, lambda j: j * LANES, LANES, 1024, LANES, F32)
    w2_pad = _pad_rows(gla_w_gate2[0], LANES)
    b_gate = gla_b_gate[0].reshape(1, GLA_KW)
    g_out = gla_g_out[0].reshape(1, GLA_VW)
    o_p, gla_state_p = gla(proj, gate_lr, w2_pad, b_gate, g_out, None, 0, n_p, len_p, 0, GLA_CHUNK, 1)
    o_s, gla_state_s = gla(proj, gate_lr, w2_pad, b_gate, g_out, state_gla, 0, n_s, len_s, t_p, len_s, 2)
    zero_conv = jnp.zeros((1, n_p, CONV_WIDTH - 1, CONV_CH), F32)
    y_p, conv_state_p = short_conv(proj, zero_conv, conv_w, 0, 0, n_p, len_p, 0, 256, 1)
    y_s, conv_state_s = short_conv(proj, state_conv, conv_w, 0, 0, n_s, len_s, t_p, CONV_CH, 2)
    x = mix_out(o_p, o_s, y_p, y_s, ab_w_out, (0,), x_p, x_s, 512, 512)
    x, h = moe_layer(x, 0, g_ffn_norm[0], *moe, g_mix_norm[1], False, t_p)

    w_mla_t = jnp.swapaxes(mla_w_in[0], 0, 1)
    c = matmul_wt(h, w_mla_t, lambda j: j * tn, Q_LORA + KV_LORA, 1024, tn, F32)
    w_pe_t = _pad_rows(w_mla_t[Q_LORA + KV_LORA:], LANES)
    kpe_raw = matmul_wt(h, w_pe_t, lambda j: j * LANES, LANES, 1024, LANES, F32)
    pos = jnp.concatenate([jnp.tile(jnp.arange(len_p), n_p), jnp.tile(past + jnp.arange(len_s), n_s)])
    cos, sin = _rope_tables(pos)
    cqn, lat, lat_b, pe, pe_b = mla_post(c, kpe_raw, cos, sin, mla_g_q[0].reshape(1, -1), mla_g_kv[0].reshape(1, -1))
    w_uq = mla_w_uq[0].reshape(Q_LORA, MLA_HEADS, NOPE + ROPE)
    w_uq_rope = jnp.pad(w_uq[:, :, NOPE:], ((0, 0), (0, 0), (0, LANES - ROPE)))
    w_uq_re = jnp.concatenate([w_uq[:, :, :NOPE].reshape(Q_LORA, -1), w_uq_rope.reshape(Q_LORA, -1)], axis=1)
    qs = q_proj(cqn, w_uq_re, cos, sin)

    kn = matmul_ws(lat_b[:t_p], mla_w_uk, (0,), 0, MLA_HEADS * NOPE, 1024, 1024, BF16)
    vv = matmul_ws(lat_b[:t_p], mla_w_uv, (0,), 0, MLA_HEADS * V_DIM, 1024, 1024, BF16)
    attn_p = flash_prompt(qs, kn, pe_b, vv, n_p, len_p)

    q_lat, q_pe = absorb_q(qs, mla_w_uk, 0, t_p, n_s, len_s)
    cache_pe_t = jnp.swapaxes(cache_k_rope, 2, 3)
    o_lat = decode_attention(page_table, q_lat, q_pe, cache_kv_latent, cache_pe_t, lat, pe, 0, t_p, len_s)
    attn_s = expand_v(o_lat, mla_w_uv, 0, len_s)
    x = matmul_ws(attn_p, mla_w_out, (0,), 0, d, 512, 512, F32, res=x, a2=attn_s)
    y_p, y_s = moe_layer(x, 1, g_ffn_norm[1], *moe, g_final_norm, True, t_p)

    return (y_p.reshape(n_p, len_p, d), y_s.reshape(n_s, len_s, d),
            gla_state_p, conv_state_p,
            lat[:t_p].reshape(1, n_p, len_p, KV_LORA), pe[:t_p, :ROPE].reshape(1, n_p, len_p, ROPE),
            gla_state_s, conv_state_s,
            lat[t_p:].reshape(1, n_s, len_s, KV_LORA), pe[t_p:, :ROPE].reshape(1, n_s, len_s, ROPE))
```
